```python
import math
import jax, jax.numpy as jnp
from jax import lax
import numpy as np

D_MODEL = 4096
BATCH = 2
SEQ = 4096
DEPTH = 2

D_FF = 11008
W_A = D_MODEL // 2
A_GROUPS = 8
CHUNK = 128
W_B = D_MODEL // 2
CONV_WIDTH = 31
W_C = D_MODEL // 2
POOL_WINDOWS = (2, 4, 8, 16)
POOL_GROUP = W_C // len(POOL_WINDOWS)
D_HEADS = 16
D_HEAD_DIM = 128
W_D = D_HEADS * D_HEAD_DIM
Q_RANK = D_MODEL // 4
KV_RANK = D_MODEL // 8
IDX_HEADS = 32
IDX_DIM = 64
TOPK_MAX = 256
QUERY_BLOCK = 128
REL_BUCKETS = 32
REL_MAX_DIST = 128

N_EVEN = (DEPTH + 1) // 2
N_ODD = DEPTH // 2
AB_IN = 2 * W_A + 2 * W_B
CD_IN = W_C + Q_RANK + KV_RANK + IDX_DIM + IDX_HEADS
EPS = 1e-6

kernel_name = 'hybrid_gmlp_conv_pool_dsa_macaron'


def rms_norm(x, g):
    xf = x.astype(jnp.float32)
    y = xf * lax.rsqrt(jnp.mean(xf * xf, axis=-1, keepdims=True) + EPS)
    return (y * g.astype(jnp.float32)).astype(x.dtype)


def layer_norm(x, g, b):
    xf = x.astype(jnp.float32)
    mu = jnp.mean(xf, axis=-1, keepdims=True)
    var = jnp.mean(jnp.square(xf - mu), axis=-1, keepdims=True)
    y = (xf - mu) * lax.rsqrt(var + EPS)
    return (y * g.astype(jnp.float32) + b.astype(jnp.float32)).astype(x.dtype)


def swiglu(h, w_in, w_out):
    gate, up = jnp.split(h @ w_in, 2, axis=-1)
    return (jax.nn.silu(gate) * up) @ w_out


def gmlp_spatial_gate(z, ln_g, ln_b, w_sp, b_sp):
    B, T, _ = z.shape
    z = jax.nn.gelu(z, approximate=False)
    u, v = z[..., :W_A], z[..., W_A:]
    v = layer_norm(v, ln_g, ln_b)
    v = v.reshape(B, T // CHUNK, CHUNK, A_GROUPS, W_A // A_GROUPS)
    mask = jnp.tril(jnp.ones((CHUNK, CHUNK), dtype=bool))
    w_m = jnp.where(mask[None], w_sp, jnp.zeros_like(w_sp))
    s = jnp.einsum('gij,bnjgc->bnigc', w_m, v) + b_sp.T[None, None, :, :, None]
    return u * s.reshape(B, T, W_A)


def causal_depthwise_conv(x, w, b):
    C = x.shape[-1]
    y = lax.conv_general_dilated(
        x, w[:, None, :].astype(x.dtype), window_strides=(1,),
        padding=[(CONV_WIDTH - 1, 0)], dimension_numbers=('NWC', 'WIO', 'NWC'),
        feature_group_count=C)
    return y + b


def conformer_conv(z, conv_w, conv_b, ln_g, ln_b):
    a, g = jnp.split(z, 2, axis=-1)
    h = a * jax.nn.sigmoid(g)
    h = causal_depthwise_conv(h, conv_w, conv_b)
    h = layer_norm(h, ln_g, ln_b)
    return jax.nn.silu(h)


def multiscale_pool(p, w_pool, scale):
    B, T, _ = p.shape
    pg = p.reshape(B, T, len(POOL_WINDOWS), POOL_GROUP).astype(jnp.float32)
    cs = jnp.cumsum(pg, axis=1)
    pos = jnp.arange(T)
    means = []
    for i, w in enumerate(POOL_WINDOWS):
        c = cs[:, :, i]
        prev = jnp.pad(c, ((0, 0), (w, 0), (0, 0)))[:, :T]
        cnt = jnp.minimum(pos + 1, w).astype(jnp.float32)[None, :, None]
        means.append((c - prev) / cnt)
    d = (jnp.stack(means, axis=2) - pg).astype(p.dtype)
    y = jnp.einsum('btgc,gce->btge', d, w_pool).reshape(B, T, W_C)
    return y * scale


def t5_bucket(n):
    n = jnp.maximum(n, 0)
    max_exact = REL_BUCKETS // 2
    nf = jnp.maximum(n, 1).astype(jnp.float32)
    large = max_exact + (jnp.log(nf / max_exact) / math.log(REL_MAX_DIST / max_exact)
                         * (REL_BUCKETS - max_exact)).astype(jnp.int32)
    large = jnp.minimum(large, REL_BUCKETS - 1)
    return jnp.where(n < max_exact, n, large)


def dsa_attention(q, q_idx, w_idx, c_kv, k_idx, w_uk, w_uv, rel_bias):
    B, T = q.shape[0], q.shape[1]
    nb = T // QUERY_BLOCK
    topk = min(TOPK_MAX, T // 4)
    spos = jnp.arange(T)
    scale = D_HEAD_DIM ** -0.5

    def to_blocks(a):
        return a.reshape((B, nb, QUERY_BLOCK) + a.shape[2:]).swapaxes(0, 1)

    def one_block(args):
        qb, qib, wb, t0 = args
        tpos = t0 + jnp.arange(QUERY_BLOCK)
        idx_logits = jnp.einsum('bqhd,bsd->bqhs', qib, k_idx)
        score = jnp.einsum('bqhs,bqh->bqs', jax.nn.relu(idx_logits).astype(jnp.float32),
                           wb.astype(jnp.float32))
        score = jnp.where(spos[None, None, :] <= tpos[None, :, None], score, -jnp.inf)
        _, sel = lax.top_k(score, topk)
        valid = sel <= tpos[None, :, None]
        kv = jax.vmap(lambda c, i: c[i])(c_kv, sel)
        q_lat = jnp.einsum('bqhd,hdr->bqhr', qb, w_uk)
        logits = jnp.einsum('bqhr,bqkr->bqkh', q_lat, kv).astype(jnp.float32) * scale
        bucket = t5_bucket(tpos[None, :, None] - sel)
        logits = logits + rel_bias[bucket].astype(jnp.float32)
        logits = jnp.where(valid[..., None], logits, -jnp.inf)
        p = jax.nn.softmax(logits, axis=2).astype(kv.dtype)
        o_lat = jnp.einsum('bqkh,bqkr->bqhr', p, kv)
        return jnp.einsum('bqhr,hrd->bqhd', o_lat, w_uv)

    out = lax.map(one_block, (to_blocks(q), to_blocks(q_idx), to_blocks(w_idx),
                              jnp.arange(nb, dtype=jnp.int32) * QUERY_BLOCK))
    return out.swapaxes(0, 1).reshape(B, T, W_D)


def mixer_ab(h, w_in, ln_a_g, ln_a_b, w_sp, b_sp, conv_w, conv_b, ln_b_g, ln_b_b, w_out):
    z = h @ w_in
    y_a = gmlp_spatial_gate(z[..., :2 * W_A], ln_a_g, ln_a_b, w_sp, b_sp)
    y_b = conformer_conv(z[..., 2 * W_A:], conv_w, conv_b, ln_b_g, ln_b_b)
    return jnp.concatenate([y_a, y_b], axis=-1) @ w_out


def mixer_cd(h, w_in, w_pool, pool_scale, g_cq, w_uq, w_qidx, g_ckv, w_uk, w_uv, rel_bias, w_out):
    B, T, _ = h.shape
    z = h @ w_in
    o1 = W_C
    o2 = o1 + Q_RANK
    o3 = o2 + KV_RANK
    o4 = o3 + IDX_DIM
    y_c = multiscale_pool(z[..., :o1], w_pool, pool_scale)
    c_q = rms_norm(z[..., o1:o2], g_cq)
    c_kv = rms_norm(z[..., o2:o3], g_ckv)
    k_idx = z[..., o3:o4]
    w_idx = z[..., o4:] * (IDX_HEADS ** -0.5 * IDX_DIM ** -0.5)
    q = (c_q @ w_uq).reshape(B, T, D_HEADS, D_HEAD_DIM)
    q_idx = (c_q @ w_qidx).reshape(B, T, IDX_HEADS, IDX_DIM)
    y_d = dsa_attention(q, q_idx, w_idx, c_kv, k_idx, w_uk, w_uv, rel_bias)
    return jnp.concatenate([y_c, y_d], axis=-1) @ w_out


def setup_inputs(seed: int = 0) -> dict:
    key = jax.random.key(seed)
    ks = jax.random.split(key, 32)
    f32 = jnp.float32

    def nrm(k, shape, fan):
        return jax.random.normal(k, shape, f32) * (fan ** -0.5)

    def gain(k, shape):
        return 1.0 + 0.05 * jax.random.normal(k, shape, f32)

    def bias(k, shape):
        return 0.02 * jax.random.normal(k, shape, f32)

    return {
        'x': jax.random.normal(ks[0], (BATCH, SEQ, D_MODEL), f32),
        'g_ff': gain(ks[1], (DEPTH, 2, D_MODEL)),
        'w_ff_in': nrm(ks[2], (DEPTH, 2, D_MODEL, 2 * D_FF), D_MODEL),
        'w_ff_out': nrm(ks[3], (DEPTH, 2, D_FF, D_MODEL), D_FF),
        'g_mix': gain(ks[4], (DEPTH, D_MODEL)),
        'w_in_ab': nrm(ks[5], (N_EVEN, D_MODEL, AB_IN), D_MODEL),
        'ln_a_g': gain(ks[6], (N_EVEN, W_A)),
        'ln_a_b': bias(ks[7], (N_EVEN, W_A)),
        'w_sp': nrm(ks[8], (N_EVEN, A_GROUPS, CHUNK, CHUNK), CHUNK),
        'b_sp': gain(ks[9], (N_EVEN, A_GROUPS, CHUNK)),
        'conv_w': nrm(ks[10], (N_EVEN, CONV_WIDTH, W_B), CONV_WIDTH),
        'conv_b': bias(ks[11], (N_EVEN, W_B)),
        'ln_b_g': gain(ks[12], (N_EVEN, W_B)),
        'ln_b_b': bias(ks[13], (N_EVEN, W_B)),
        'w_out_ab': nrm(ks[14], (N_EVEN, W_A + W_B, D_MODEL), W_A + W_B),
        'w_in_cd': nrm(ks[15], (N_ODD, D_MODEL, CD_IN), D_MODEL),
        'w_pool': nrm(ks[16], (N_ODD, len(POOL_WINDOWS), POOL_GROUP, POOL_GROUP), POOL_GROUP),
        'pool_scale': gain(ks[17], (N_ODD, W_C)),
        'g_cq': gain(ks[18], (N_ODD, Q_RANK)),
        'w_uq': nrm(ks[19], (N_ODD, Q_RANK, W_D), Q_RANK),
        'w_qidx': nrm(ks[20], (N_ODD, Q_RANK, IDX_HEADS * IDX_DIM), Q_RANK),
        'g_ckv': gain(ks[21], (N_ODD, KV_RANK)),
        'w_uk': nrm(ks[22], (N_ODD, D_HEADS, D_HEAD_DIM, KV_RANK), KV_RANK),
        'w_uv': nrm(ks[23], (N_ODD, D_HEADS, KV_RANK, D_HEAD_DIM), KV_RANK),
        'rel_bias': 0.5 * jax.random.normal(ks[24], (REL_BUCKETS, D_HEADS), f32),
        'w_out_cd': nrm(ks[25], (N_ODD, W_C + W_D, D_MODEL), W_C + W_D),
        'g_final': gain(ks[26], (D_MODEL,)),
    }


def reference(x, g_ff, w_ff_in, w_ff_out, g_mix, w_in_ab, ln_a_g, ln_a_b, w_sp, b_sp,
              conv_w, conv_b, ln_b_g, ln_b_b, w_out_ab, w_in_cd, w_pool, pool_scale,
              g_cq, w_uq, w_qidx, g_ckv, w_uk, w_uv, rel_bias, w_out_cd, g_final):
    for l in range(DEPTH):
        x = x + 0.5 * swiglu(rms_norm(x, g_ff[l, 0]), w_ff_in[l, 0], w_ff_out[l, 0])
        h = rms_norm(x, g_mix[l])
        i = l // 2
        if l % 2 == 0:
            x = x + mixer_ab(h, w_in_ab[i], ln_a_g[i], ln_a_b[i], w_sp[i], b_sp[i],
                             conv_w[i], conv_b[i], ln_b_g[i], ln_b_b[i], w_out_ab[i])
        else:
            x = x + mixer_cd(h, w_in_cd[i], w_pool[i], pool_scale[i], g_cq[i], w_uq[i],
                             w_qidx[i], g_ckv[i], w_uk[i], w_uv[i], rel_bias, w_out_cd[i])
        x = x + 0.5 * swiglu(rms_norm(x, g_ff[l, 1]), w_ff_in[l, 1], w_ff_out[l, 1])
    return rms_norm(x, g_final)
```

```python
import functools
import math

import numpy as np
import jax
import jax.numpy as jnp
from jax import lax
from jax.experimental import pallas as pl
from jax.experimental.pallas import tpu as pltpu

F32 = jnp.float32
BF16 = jnp.bfloat16

EPS = 1e-6
CHUNK = 128
CONV_WIDTH = 31
POOL_WINDOWS = (2, 4, 8, 16)
TOPK_MAX = 256
REL_BUCKETS = 32
REL_MAX_DIST = 128
QUERY_TILE = 128
FAR_BLOCK = 512
NEG = -1e30
INT_MIN = -2147483648

V7X_VMEM_BYTES = 64 * 1024 * 1024
VMEM_LIMIT = V7X_VMEM_BYTES - 8 * 1024 * 1024

_NT = (((1,), (1,)), ((), ()))


def _params(*sem):
    return pltpu.CompilerParams(dimension_semantics=sem, vmem_limit_bytes=VMEM_LIMIT)


def _row_tile(n, want):
    t = min(n, want)
    assert n % t == 0, (n, t)
    return t


def _rms_kernel(x_ref, g_ref, o_ref):
    x = x_ref[...].astype(F32)
    y = x * lax.rsqrt(jnp.mean(x * x, axis=-1, keepdims=True) + EPS)
    o_ref[...] = (y * g_ref[...]).astype(o_ref.dtype)


def rms_norm(x, g, out_dtype, *, col_block=0, width=None, tm=256):
    n = x.shape[0]
    width = x.shape[1] if width is None else width
    tm = _row_tile(n, tm)
    return pl.pallas_call(
        _rms_kernel,
        grid=(n // tm,),
        in_specs=[pl.BlockSpec((tm, width), lambda i: (i, col_block)),
                  pl.BlockSpec((1, width), lambda i: (0, 0))],
        out_specs=pl.BlockSpec((tm, width), lambda i: (i, 0)),
        out_shape=jax.ShapeDtypeStruct((n, width), out_dtype),
        compiler_params=_params("parallel"),
        name="rms_norm",
    )(x, g.reshape(1, width).astype(F32))


def _ln_silu_kernel(x_ref, g_ref, b_ref, o_ref):
    x = x_ref[...]
    mu = jnp.mean(x, axis=-1, keepdims=True)
    xc = x - mu
    var = jnp.mean(xc * xc, axis=-1, keepdims=True)
    y = xc * lax.rsqrt(var + EPS) * g_ref[...] + b_ref[...]
    o_ref[...] = (y * jax.nn.sigmoid(y)).astype(o_ref.dtype)


def ln_silu(x, g, b, out_dtype, *, tm=256):
    n, d = x.shape
    tm = _row_tile(n, tm)
    return pl.pallas_call(
        _ln_silu_kernel,
        grid=(n // tm,),
        in_specs=[pl.BlockSpec((tm, d), lambda i: (i, 0)),
                  pl.BlockSpec((1, d), lambda i: (0, 0)),
                  pl.BlockSpec((1, d), lambda i: (0, 0))],
        out_specs=pl.BlockSpec((tm, d), lambda i: (i, 0)),
        out_shape=jax.ShapeDtypeStruct((n, d), out_dtype),
        compiler_params=_params("parallel"),
        name="ln_silu",
    )(x, g.reshape(1, d), b.reshape(1, d))


def _mm_kernel(a_ref, w_ref, o_ref, *, act):
    r = jnp.dot(a_ref[...], w_ref[...], preferred_element_type=F32)
    if act == "gelu":
        r = 0.5 * r * (1.0 + lax.erf(r * math.sqrt(0.5)))
    o_ref[...] = r.astype(o_ref.dtype)


def matmul(a, w, out_dtype, *, act=None, tm=1024, tn=512):
    n, k = a.shape
    nout = w.shape[1]
    tm = _row_tile(n, tm)
    tn = _row_tile(nout, tn)
    return pl.pallas_call(
        functools.partial(_mm_kernel, act=act),
        grid=(n // tm, nout // tn),
        in_specs=[pl.BlockSpec((tm, k), lambda i, j: (i, 0)),
                  pl.BlockSpec((k, tn), lambda i, j: (0, j))],
        out_specs=pl.BlockSpec((tm, tn), lambda i, j: (i, j)),
        out_shape=jax.ShapeDtypeStruct((n, nout), out_dtype),
        compiler_params=_params("parallel", "arbitrary"),
        name="matmul_" + (act or "plain"),
    )(a, w)


def _dual_mm_kernel(a_ref, w1_ref, w2_ref, o_ref, *, act):
    a = a_ref[...]
    p = jnp.dot(a, w1_ref[...], preferred_element_type=F32)
    q = jnp.dot(a, w2_ref[...], preferred_element_type=F32)
    if act == "swiglu":
        r = p * jax.nn.sigmoid(p) * q
    else:
        r = p * jax.nn.sigmoid(q)
    o_ref[...] = r.astype(o_ref.dtype)


def dual_matmul(a, w1, w2, out_dtype, *, act, tm=1024, tn=256):
    n, k = a.shape
    nout = w1.shape[1]
    tm = _row_tile(n, tm)
    tn = _row_tile(nout, tn)
    return pl.pallas_call(
        functools.partial(_dual_mm_kernel, act=act),
        grid=(n // tm, nout // tn),
        in_specs=[pl.BlockSpec((tm, k), lambda i, j: (i, 0)),
                  pl.BlockSpec((k, tn), lambda i, j: (0, j)),
                  pl.BlockSpec((k, tn), lambda i, j: (0, j))],
        out_specs=pl.BlockSpec((tm, tn), lambda i, j: (i, j)),
        out_shape=jax.ShapeDtypeStruct((n, nout), out_dtype),
        compiler_params=_params("parallel", "arbitrary"),
        name="dual_matmul_" + act,
    )(a, w1, w2)


def _mm_res_kernel(*refs, npairs, scale):
    res_ref = refs[2 * npairs]
    o_ref = refs[2 * npairs + 1]
    acc = jnp.dot(refs[0][...], refs[1][...], preferred_element_type=F32)
    for p in range(1, npairs):
        acc = acc + jnp.dot(refs[2 * p][...], refs[2 * p + 1][...], preferred_element_type=F32)
    o_ref[...] = res_ref[...] + scale * acc


def matmul_residual(pairs, res, scale, *, tm, tn):
    n, nout = res.shape
    tm = _row_tile(n, tm)
    tn = _row_tile(nout, tn)
    in_specs, args = [], []
    for a, w in pairs:
        k = a.shape[1]
        in_specs += [pl.BlockSpec((tm, k), lambda i, j: (i, 0)),
                     pl.BlockSpec((k, tn), lambda i, j: (0, j))]
        args += [a, w]
    in_specs.append(pl.BlockSpec((tm, tn), lambda i, j: (i, j)))
    args.append(res)
    return pl.pallas_call(
        functools.partial(_mm_res_kernel, npairs=len(pairs), scale=scale),
        grid=(n // tm, nout // tn),
        in_specs=in_specs,
        out_specs=pl.BlockSpec((tm, tn), lambda i, j: (i, j)),
        out_shape=jax.ShapeDtypeStruct((n, nout), F32),
        compiler_params=_params("parallel", "arbitrary"),
        name="matmul_residual",
    )(*args)


def _spatial_gate_kernel(u_ref, v_ref, g_ref, b_ref, wsp_ref, bspt_ref, o_ref, *, groups, rows):
    v = v_ref[...]
    mu = jnp.mean(v, axis=-1, keepdims=True)
    vc = v - mu
    var = jnp.mean(vc * vc, axis=-1, keepdims=True)
    vn = (vc * lax.rsqrt(var + EPS) * g_ref[...] + b_ref[...]).astype(BF16)
    gw = vn.shape[1] // groups
    ri = lax.broadcasted_iota(jnp.int32, (CHUNK, CHUNK), 0)
    ci = lax.broadcasted_iota(jnp.int32, (CHUNK, CHUNK), 1)
    tril = ci <= ri
    bspt = bspt_ref[...]
    for g in range(groups):
        wm = jnp.where(tril, wsp_ref[g], 0.0).astype(BF16)
        bias = bspt[:, g:g + 1]
        for c in range(rows // CHUNK):
            rs = slice(c * CHUNK, (c + 1) * CHUNK)
            cs = slice(g * gw, (g + 1) * gw)
            s = jnp.dot(wm, vn[rs, cs], preferred_element_type=F32) + bias
            o_ref[rs, cs] = (u_ref[rs, cs] * s).astype(o_ref.dtype)


def spatial_gate(uv, ln_g, ln_b, w_sp, b_sp, *, rows=512):
    n = uv.shape[0]
    w = uv.shape[1] // 2
    groups = w_sp.shape[0]
    rows = _row_tile(n, rows)
    return pl.pallas_call(
        functools.partial(_spatial_gate_kernel, groups=groups, rows=rows),
        grid=(n // rows,),
        in_specs=[pl.BlockSpec((rows, w), lambda i: (i, 0)),
                  pl.BlockSpec((rows, w), lambda i: (i, 1)),
                  pl.BlockSpec((1, w), lambda i: (0, 0)),
                  pl.BlockSpec((1, w), lambda i: (0, 0)),
                  pl.BlockSpec((groups, CHUNK, CHUNK), lambda i: (0, 0, 0)),
                  pl.BlockSpec((CHUNK, groups), lambda i: (0, 0))],
        out_specs=pl.BlockSpec((rows, w), lambda i: (i, 0)),
        out_shape=jax.ShapeDtypeStruct((n, w), BF16),
        compiler_params=_params("parallel"),
        name="spatial_gate",
    )(uv, uv, ln_g.reshape(1, w), ln_b.reshape(1, w), w_sp, b_sp.T)


HALO = 32


def _conv_kernel(cur_ref, halo_ref, w_ref, b_ref, o_ref, xpad_ref, *, tt, rb):
    t = pl.program_id(1)
    halo = halo_ref[0]
    xpad_ref[0:HALO, :] = jnp.where(t == 0, jnp.zeros_like(halo), halo)
    xpad_ref[HALO:HALO + tt, :] = cur_ref[0]
    bias = b_ref[...]
    base = HALO - (CONV_WIDTH - 1)
    for r in range(tt // rb):
        acc = jnp.zeros((rb, bias.shape[1]), F32) + bias
        for k in range(CONV_WIDTH):
            acc = acc + xpad_ref[pl.ds(r * rb + base + k, rb), :] * w_ref[pl.ds(k, 1), :]
        o_ref[0, r * rb:(r + 1) * rb, :] = acc


def causal_conv(x, w, b, *, tt=512, cb=256, rb=64):
    bsz, t, c = x.shape
    tt = _row_tile(t, tt)
    return pl.pallas_call(
        functools.partial(_conv_kernel, tt=tt, rb=rb),
        grid=(bsz, t // tt, c // cb),
        in_specs=[pl.BlockSpec((1, tt, cb), lambda bi, ti, ci: (bi, ti, ci)),
                  pl.BlockSpec((1, HALO, cb),
                               lambda bi, ti, ci: (bi, jnp.maximum(ti * (tt // HALO) - 1, 0), ci)),
                  pl.BlockSpec((CONV_WIDTH, cb), lambda bi, ti, ci: (0, ci)),
                  pl.BlockSpec((1, cb), lambda bi, ti, ci: (0, ci))],
        out_specs=pl.BlockSpec((1, tt, cb), lambda bi, ti, ci: (bi, ti, ci)),
        out_shape=jax.ShapeDtypeStruct((bsz, t, c), F32),
        scratch_shapes=[pltpu.VMEM((HALO + tt, cb), F32)],
        compiler_params=_params("parallel", "parallel", "parallel"),
        name="causal_conv",
    )(x, x, w, b.reshape(1, c))


PHALO = 16


def _pool_kernel(cur_ref, halo_ref, w_ref, sc_ref, o_ref, xpad_ref, *, tt, gw):
    t = pl.program_id(1)
    halo = halo_ref[0]
    xpad_ref[0:PHALO, :] = jnp.where(t == 0, jnp.zeros_like(halo), halo)
    xpad_ref[PHALO:PHALO + tt, :] = cur_ref[0]
    pos = t * tt + lax.broadcasted_iota(jnp.int32, (tt, 1), 0)
    for g, win in enumerate(POOL_WINDOWS):
        cs = slice(g * gw, (g + 1) * gw)
        p = xpad_ref[PHALO:PHALO + tt, cs]
        s = p
        for j in range(1, win):
            s = s + xpad_ref[PHALO - j:PHALO - j + tt, cs]
        cnt = jnp.minimum(pos + 1, win).astype(F32)
        d = (s / cnt - p).astype(BF16)
        y = jnp.dot(d, w_ref[g], preferred_element_type=F32) * sc_ref[:, cs]
        o_ref[0, :, cs] = y.astype(o_ref.dtype)


def multiscale_pool(z, w_pool, scale, *, width, tt=256):
    bsz, t, _ = z.shape
    groups = len(POOL_WINDOWS)
    gw = width // groups
    tt = _row_tile(t, tt)
    return pl.pallas_call(
        functools.partial(_pool_kernel, tt=tt, gw=gw),
        grid=(bsz, t // tt),
        in_specs=[pl.BlockSpec((1, tt, width), lambda bi, ti: (bi, ti, 0)),
                  pl.BlockSpec((1, PHALO, width),
                               lambda bi, ti: (bi, jnp.maximum(ti * (tt // PHALO) - 1, 0), 0)),
                  pl.BlockSpec((groups, gw, gw), lambda bi, ti: (0, 0, 0)),
                  pl.BlockSpec((1, width), lambda bi, ti: (0, 0))],
        out_specs=pl.BlockSpec((1, tt, width), lambda bi, ti: (bi, ti, 0)),
        out_shape=jax.ShapeDtypeStruct((bsz, t, width), BF16),
        scratch_shapes=[pltpu.VMEM((PHALO + tt, width), F32)],
        compiler_params=_params("parallel", "parallel"),
        name="multiscale_pool",
    )(z, z, w_pool, scale.reshape(1, width))


def _select_kernel(q_ref, w_ref, ke_ref, ko_ref, o_ref, key_ref, *, tq, ts, nkb, npair, topk, wscale):
    qi = pl.program_id(1)
    t0 = qi * tq
    nsb = (t0 + tq + ts - 1) // ts
    w = w_ref[0] * wscale
    trow = t0 + lax.broadcasted_iota(jnp.int32, (tq, ts), 0)
    scol = lax.broadcasted_iota(jnp.int32, (tq, ts), 1)

    def score_body(kb, c):
        s0 = pl.multiple_of(kb * ts, ts)
        ke = ke_ref[0, pl.ds(s0, ts), :]
        ko = ko_ref[0, pl.ds(s0, ts), :]
        acc = jnp.zeros((tq, ts), F32)
        for j in range(npair):
            slab = q_ref[0, :, j * 128:(j + 1) * 128]
            le = lax.dot_general(slab, ke, _NT, preferred_element_type=F32)
            lo = lax.dot_general(slab, ko, _NT, preferred_element_type=F32)
            acc = acc + jnp.maximum(le, 0.0) * w[:, 2 * j:2 * j + 1]
            acc = acc + jnp.maximum(lo, 0.0) * w[:, 2 * j + 1:2 * j + 2]
        acc = jnp.where(s0 + scol <= trow, acc, -jnp.inf)
        bits = pltpu.bitcast(acc, jnp.int32)
        key_ref[kb] = jnp.where(bits < 0, bits ^ jnp.int32(0x7FFFFFFF), bits)
        return c

    lax.fori_loop(0, nsb, score_body, 0)

    def bit_body(it, tu):
        bit = jnp.left_shift(jnp.int32(1), 31 - it)
        cand_u = tu | bit
        cand = jnp.concatenate([cand_u ^ jnp.int32(INT_MIN)] * (ts // 128), axis=1)

        def count_body(kb, cnt):
            ge = jnp.where(key_ref[kb] >= cand, 1.0, 0.0)
            for l in range(ts // 128):
                cnt = cnt + ge[:, l * 128:(l + 1) * 128]
            return cnt

        cnt = lax.fori_loop(0, nsb, count_body, jnp.zeros((tq, 128), F32))
        total = jnp.sum(cnt, axis=1, keepdims=True)
        return jnp.where(total >= float(topk), cand_u, tu)

    tu = lax.fori_loop(0, 32, bit_body, jnp.zeros((tq, 128), jnp.int32))
    thr = jnp.concatenate([tu ^ jnp.int32(INT_MIN)] * (ts // 128), axis=1)

    for kb in range(nkb):
        cols = slice(kb * ts, (kb + 1) * ts)

        @pl.when(kb < nsb)
        def _():
            sel = (key_ref[kb] >= thr) & (kb * ts + scol <= trow)
            o_ref[0, :, cols] = jnp.where(sel, 0.0, NEG).astype(o_ref.dtype)

        @pl.when(kb >= nsb)
        def _():
            o_ref[0, :, cols] = jnp.full((tq, ts), NEG, o_ref.dtype)


def select_mask(q_idx, w_idx, k_even, k_odd, *, n_heads, topk, tq=QUERY_TILE, ts=FAR_BLOCK):
    bsz, t, hd = q_idx.shape
    ts = _row_tile(t, ts)
    nkb = t // ts
    npair = hd // 128
    wscale = float(n_heads ** -0.5 * (hd // n_heads) ** -0.5)
    return pl.pallas_call(
        functools.partial(_select_kernel, tq=tq, ts=ts, nkb=nkb, npair=npair, topk=topk, wscale=wscale),
        grid=(bsz, t // tq),
        in_specs=[pl.BlockSpec((1, tq, hd), lambda b, i: (b, i, 0)),
                  pl.BlockSpec((1, tq, 128), lambda b, i: (b, i, 0)),
                  pl.BlockSpec((1, t, 128), lambda b, i: (b, 0, 0)),
                  pl.BlockSpec((1, t, 128), lambda b, i: (b, 0, 0))],
        out_specs=pl.BlockSpec((1, tq, t), lambda b, i: (b, i, 0)),
        out_shape=jax.ShapeDtypeStruct((bsz, t, t), BF16),
        scratch_shapes=[pltpu.VMEM((nkb, tq, ts), jnp.int32)],
        compiler_params=_params("parallel", "arbitrary"),
        name="select_mask",
    )(q_idx, w_idx, k_even, k_odd)


def _t5_bucket_table(n_max):
    n = np.arange(n_max)
    max_exact = REL_BUCKETS // 2
    nf = np.maximum(n, 1).astype(np.float32)
    ratio = np.log(nf / np.float32(max_exact)) / np.float32(math.log(REL_MAX_DIST / max_exact))
    large = max_exact + (ratio * np.float32(REL_BUCKETS - max_exact)).astype(np.int32)
    large = np.minimum(large, REL_BUCKETS - 1)
    return np.where(n < max_exact, n, large).astype(np.int32)


def _bias_kernel(rb_ref, o_ref, *, tq, thresholds):
    i = lax.broadcasted_iota(jnp.int32, (tq, 2 * tq), 0)
    j = lax.broadcasted_iota(jnp.int32, (tq, 2 * tq), 1)
    nh = o_ref.shape[1]
    for v in range(2):
        d = jnp.maximum(v * tq + i - j, 0)
        bucket = jnp.zeros_like(d)
        for th in thresholds:
            bucket = bucket + jnp.where(d >= th, 1, 0)
        for h in range(nh):
            val = jnp.zeros((tq, 2 * tq), F32)
            for b in range(REL_BUCKETS):
                val = jnp.where(bucket == b, rb_ref[b, h] - rb_ref[REL_BUCKETS - 1, h], val)
            o_ref[v, h] = val


def bias_windows(rel_bias, *, tq=QUERY_TILE):
    table = _t5_bucket_table(4 * tq)
    assert np.all(np.diff(table) >= 0) and np.all(np.diff(table) <= 1)
    assert np.all(table[tq:] == REL_BUCKETS - 1), "bias must be constant beyond the near window"
    thresholds = tuple(int(np.argmax(table >= b)) for b in range(1, REL_BUCKETS))
    nh = rel_bias.shape[1]
    return pl.pallas_call(
        functools.partial(_bias_kernel, tq=tq, thresholds=thresholds),
        in_specs=[pl.BlockSpec(memory_space=pltpu.SMEM)],
        out_specs=pl.BlockSpec(memory_space=pltpu.VMEM),
        out_shape=jax.ShapeDtypeStruct((2, nh, tq, 2 * tq), F32),
        compiler_params=pltpu.CompilerParams(vmem_limit_bytes=VMEM_LIMIT),
        name="bias_windows",
    )(rel_bias)


def _attn_kernel(q_ref, madd_ref, ckv_ref, wuk_ref, wuv_ref, btab_ref, o_ref,
                 qlat_ref, m_ref, l_ref, acc_ref, *, tq, nh, dh, fb, scale):
    qi = pl.program_id(1)
    t0 = qi * tq
    for h in range(nh):
        qh = q_ref[0, :, h * dh:(h + 1) * dh]
        qlat_ref[h] = jnp.dot(qh, wuk_ref[h], preferred_element_type=F32).astype(BF16)
    m_ref[...] = jnp.full(m_ref.shape, NEG, F32)
    l_ref[...] = jnp.zeros(l_ref.shape, F32)
    acc_ref[...] = jnp.zeros(acc_ref.shape, F32)

    def block(kv, ma, bias_of_head):
        for h in range(nh):
            lg = lax.dot_general(qlat_ref[h], kv, _NT, preferred_element_type=F32) * scale + ma
            if bias_of_head is not None:
                lg = lg + bias_of_head(h)
            m_old = m_ref[h][:, :1]
            m_new = jnp.maximum(m_old, jnp.max(lg, axis=1, keepdims=True))
            alpha = jnp.exp(m_old - m_new)
            p = jnp.exp(lg - m_new)
            l_new = alpha * l_ref[h][:, :1] + jnp.sum(p, axis=1, keepdims=True)
            acc_ref[h] = alpha * acc_ref[h] + jnp.dot(p.astype(BF16), kv, preferred_element_type=F32)
            m_ref[h] = jnp.broadcast_to(m_new, (tq, 128))
            l_ref[h] = jnp.broadcast_to(l_new, (tq, 128))

    lim = t0 - tq
    nfar = (jnp.maximum(lim, 0) + fb - 1) // fb
    scol = lax.broadcasted_iota(jnp.int32, (tq, fb), 1)

    def far_body(kb, c):
        s0 = pl.multiple_of(kb * fb, fb)
        kv = ckv_ref[0, pl.ds(s0, fb), :]
        ma = madd_ref[0, :, pl.ds(s0, fb)].astype(F32)
        ma = jnp.where(s0 + scol < lim, ma, NEG)
        block(kv, ma, None)
        return c

    lax.fori_loop(0, nfar, far_body, 0)

    w0 = pl.multiple_of(jnp.maximum(lim, 0), tq)
    variant = jnp.minimum(qi, 1)
    kv = ckv_ref[0, pl.ds(w0, 2 * tq), :]
    ma = madd_ref[0, :, pl.ds(w0, 2 * tq)].astype(F32)
    block(kv, ma, lambda h: btab_ref[variant, h])

    for h in range(nh):
        o = acc_ref[h] / l_ref[h][:, :1]
        y = jnp.dot(o.astype(BF16), wuv_ref[h], preferred_element_type=F32)
        o_ref[0, :, h * dh:(h + 1) * dh] = y.astype(o_ref.dtype)


def masked_latent_attention(q, madd, c_kv, w_uk, w_uv, btab, *, tq=QUERY_TILE, fb=FAR_BLOCK):
    bsz, t, hd = q.shape
    nh, dh, r = w_uk.shape
    fb = _row_tile(t, fb)
    assert t >= 2 * tq and t % tq == 0
    return pl.pallas_call(
        functools.partial(_attn_kernel, tq=tq, nh=nh, dh=dh, fb=fb, scale=float(dh ** -0.5)),
        grid=(bsz, t // tq),
        in_specs=[pl.BlockSpec((1, tq, hd), lambda b, i: (b, i, 0)),
                  pl.BlockSpec((1, tq, t), lambda b, i: (b, i, 0)),
                  pl.BlockSpec((1, t, r), lambda b, i: (b, 0, 0)),
                  pl.BlockSpec((nh, dh, r), lambda b, i: (0, 0, 0)),
                  pl.BlockSpec((nh, r, dh), lambda b, i: (0, 0, 0)),
                  pl.BlockSpec((2, nh, tq, 2 * tq), lambda b, i: (0, 0, 0, 0))],
        out_specs=pl.BlockSpec((1, tq, hd), lambda b, i: (b, i, 0)),
        out_shape=jax.ShapeDtypeStruct((bsz, t, hd), BF16),
        scratch_shapes=[pltpu.VMEM((nh, tq, r), BF16),
                        pltpu.VMEM((nh, tq, 128), F32),
                        pltpu.VMEM((nh, tq, 128), F32),
                        pltpu.VMEM((nh, tq, r), F32)],
        compiler_params=_params("parallel", "arbitrary"),
        name="masked_latent_attention",
    )(q, madd, c_kv, w_uk, w_uv, btab)


def _ffn(x, g, w_in, w_out):
    dff = w_out.shape[0]
    h = rms_norm(x, g, BF16)
    a = dual_matmul(h, w_in[:, :dff].astype(BF16), w_in[:, dff:].astype(BF16), BF16,
                    act="swiglu", tm=1024, tn=256)
    return matmul_residual([(a, w_out.astype(BF16))], x, 0.5, tm=512, tn=512)


def _mixer_ab(x, h, bsz, t, w_in, ln_a_g, ln_a_b, w_sp, b_sp, conv_w, conv_b, ln_b_g, ln_b_b, w_out):
    wa = ln_a_g.shape[0]
    wb = ln_b_g.shape[0]
    uv = matmul(h, w_in[:, :2 * wa].astype(BF16), F32, act="gelu", tm=1024, tn=512)
    y_a = spatial_gate(uv, ln_a_g, ln_a_b, w_sp, b_sp)
    glu = dual_matmul(h, w_in[:, 2 * wa:2 * wa + wb].astype(BF16), w_in[:, 2 * wa + wb:].astype(BF16),
                      F32, act="glu", tm=1024, tn=256)
    conv = causal_conv(glu.reshape(bsz, t, wb), conv_w, conv_b).reshape(bsz * t, wb)
    y_b = ln_silu(conv, ln_b_g, ln_b_b, BF16)
    w_out = w_out.astype(BF16)
    return matmul_residual([(y_a, w_out[:wa]), (y_b, w_out[wa:])], x, 1.0, tm=1024, tn=512)


def _mixer_cd(x, h, bsz, t, w_in, w_pool, pool_scale, g_cq, w_uq, w_qidx, g_ckv, w_uk, w_uv,
              rel_bias, w_out):
    n = bsz * t
    wc = pool_scale.shape[0]
    qr = g_cq.shape[0]
    kvr = g_ckv.shape[0]
    n_heads, dh, _ = w_uk.shape
    idx_dim = 64
    idx_heads = w_qidx.shape[1] // idx_dim
    o3 = wc + qr + kvr
    assert wc % qr == 0 and (wc + qr) % kvr == 0 and idx_heads <= 128 and 2 * idx_dim == 128

    z = matmul(h, w_in[:, :o3].astype(BF16), F32, tm=1024, tn=512)
    wk = w_in[:, o3:o3 + idx_dim]
    ww = w_in[:, o3 + idx_dim:]
    zk = jnp.zeros_like(wk)
    w_small = jnp.concatenate(
        [wk, zk, zk, wk, ww, jnp.zeros((w_in.shape[0], 128 - idx_heads), w_in.dtype)], axis=1)
    zs = matmul(h, w_small.astype(BF16), F32, tm=1024, tn=384)
    k_even = zs[:, :128].astype(BF16).reshape(bsz, t, 128)
    k_odd = zs[:, 128:256].astype(BF16).reshape(bsz, t, 128)
    w_idx = zs[:, 256:].reshape(bsz, t, 128)

    y_c = multiscale_pool(z.reshape(bsz, t, o3), w_pool.astype(BF16), pool_scale, width=wc)
    c_q = rms_norm(z, g_cq, BF16, col_block=wc // qr, width=qr)
    c_kv = rms_norm(z, g_ckv, BF16, col_block=(wc + qr) // kvr, width=kvr)
    q = matmul(c_q, w_uq.astype(BF16), BF16, tm=1024, tn=512)
    q_idx = matmul(c_q, w_qidx.astype(BF16), BF16, tm=1024, tn=512)

    topk = min(TOPK_MAX, t // 4)
    madd = select_mask(q_idx.reshape(bsz, t, -1), w_idx, k_even, k_odd, n_heads=idx_heads, topk=topk)
    btab = bias_windows(rel_bias)
    y_d = masked_latent_attention(q.reshape(bsz, t, -1), madd, c_kv.reshape(bsz, t, kvr),
                                  w_uk.astype(BF16), w_uv.astype(BF16), btab)
    w_out = w_out.astype(BF16)
    return matmul_residual([(y_c.reshape(n, wc), w_out[:wc]), (y_d.reshape(n, -1), w_out[wc:])],
                           x, 1.0, tm=1024, tn=512)


def kernel(x, g_ff, w_ff_in, w_ff_out, g_mix, w_in_ab, ln_a_g, ln_a_b, w_sp, b_sp, conv_w, conv_b,
           ln_b_g, ln_b_b, w_out_ab, w_in_cd, w_pool, pool_scale, g_cq, w_uq, w_qidx, g_ckv, w_uk,
           w_uv, rel_bias, w_out_cd, g_final):
    bsz, t, d = x.shape
    depth = g_ff.shape[0]
    xs = x.reshape(bsz * t, d)
    for l in range(depth):
        xs = _ffn(xs, g_ff[l, 0], w_ff_in[l, 0], w_ff_out[l, 0])
        h = rms_norm(xs, g_mix[l], BF16)
        i = l // 2
        if l % 2 == 0:
            xs = _mixer_ab(xs, h, bsz, t, w_in_ab[i], ln_a_g[i], ln_a_b[i], w_sp[i], b_sp[i],
                           conv_w[i], conv_b[i], ln_b_g[i], ln_b_b[i], w_out_ab[i])
        else:
            xs = _mixer_cd(xs, h, bsz, t, w_in_cd[i], w_pool[i], pool_scale[i], g_cq[i], w_uq[i],
                           w_qidx[i], g_ckv[i], w_uk[i], w_uv[i], rel_bias, w_out_cd[i])
        xs = _ffn(xs, g_ff[l, 1], w_ff_in[l, 1], w_ff_out[l, 1])
    return rms_norm(xs, g_final, x.dtype).reshape(bsz, t, d)
```

```python
import functools
import math
from typing import NamedTuple, Optional

import numpy as np
import jax
import jax.numpy as jnp
from jax import lax
from jax.experimental import pallas as pl
from jax.experimental.pallas import tpu as pltpu

F32 = jnp.float32
BF16 = jnp.bfloat16

EPS = 1e-6
CHUNK = 128
CONV_WIDTH = 31
POOL_WINDOWS = (2, 4, 8, 16)
TOPK_MAX = 256
REL_BUCKETS = 32
REL_MAX_DIST = 128
QUERY_TILE = 128
FAR_BLOCK = 512
NEG = -1e30
INT_MIN = -2147483648

V7X_VMEM_BYTES = 64 * 1024 * 1024
VMEM_LIMIT = V7X_VMEM_BYTES - 8 * 1024 * 1024

_NT = (((1,), (1,)), ((), ()))


def _params(*sem):
    return pltpu.CompilerParams(dimension_semantics=sem, vmem_limit_bytes=VMEM_LIMIT)


def _row_tile(n, want):
    t = min(n, want)
    assert n % t == 0, (n, t)
    return t


def _rms_kernel(x_ref, g_ref, o_ref):
    x = x_ref[...].astype(F32)
    y = x * lax.rsqrt(jnp.mean(x * x, axis=-1, keepdims=True) + EPS)
    o_ref[...] = (y * g_ref[...]).astype(o_ref.dtype)


def rms_norm(x, g, out_dtype, *, col_block=0, width=None, tm=256):
    n = x.shape[0]
    width = x.shape[1] if width is None else width
    tm = _row_tile(n, tm)
    return pl.pallas_call(
        _rms_kernel,
        grid=(n // tm,),
        in_specs=[pl.BlockSpec((tm, width), lambda i: (i, col_block)),
                  pl.BlockSpec((1, width), lambda i: (0, 0))],
        out_specs=pl.BlockSpec((tm, width), lambda i: (i, 0)),
        out_shape=jax.ShapeDtypeStruct((n, width), out_dtype),
        compiler_params=_params("parallel"),
        name="rms_norm",
    )(x, g.reshape(1, width).astype(F32))


def _ln_silu_kernel(x_ref, g_ref, b_ref, o_ref):
    x = x_ref[...]
    mu = jnp.mean(x, axis=-1, keepdims=True)
    xc = x - mu
    var = jnp.mean(xc * xc, axis=-1, keepdims=True)
    y = xc * lax.rsqrt(var + EPS) * g_ref[...] + b_ref[...]
    o_ref[...] = (y * jax.nn.sigmoid(y)).astype(o_ref.dtype)


def ln_silu(x, g, b, out_dtype, *, tm=256):
    n, d = x.shape
    tm = _row_tile(n, tm)
    return pl.pallas_call(
        _ln_silu_kernel,
        grid=(n // tm,),
        in_specs=[pl.BlockSpec((tm, d), lambda i: (i, 0)),
                  pl.BlockSpec((1, d), lambda i: (0, 0)),
                  pl.BlockSpec((1, d), lambda i: (0, 0))],
        out_specs=pl.BlockSpec((tm, d), lambda i: (i, 0)),
        out_shape=jax.ShapeDtypeStruct((n, d), out_dtype),
        compiler_params=_params("parallel"),
        name="ln_silu",
    )(x, g.reshape(1, d), b.reshape(1, d))


class WSlab(NamedTuple):
    arr: jax.Array
    lead: tuple = ()
    row0: int = 0
    rows: Optional[int] = None
    col0: int = 0
    cols: Optional[int] = None

    def shape(self):
        r = self.arr.shape[-2] if self.rows is None else self.rows
        c = self.arr.shape[-1] if self.cols is None else self.cols
        return r, c

    def spec(self, tn):
        r, _ = self.shape()
        assert self.row0 % r == 0 and self.col0 % tn == 0, (self.row0, r, self.col0, tn)
        lead, rb, cb = self.lead, self.row0 // r, self.col0 // tn
        return pl.BlockSpec((None,) * len(lead) + (r, tn), lambda i, j: lead + (rb, cb + j))


def _mm_kernel(a_ref, w_ref, o_ref, *, act):
    r = jnp.dot(a_ref[...], w_ref[...].astype(BF16), preferred_element_type=F32)
    if act == "gelu":
        r = 0.5 * r * (1.0 + lax.erf(r * math.sqrt(0.5)))
    o_ref[...] = r.astype(o_ref.dtype)


def matmul(a, w, out_dtype, *, act=None, tm=1024, tn=512):
    n, k = a.shape
    assert w.shape()[0] == k
    nout = w.shape()[1]
    tm = _row_tile(n, tm)
    tn = _row_tile(nout, tn)
    return pl.pallas_call(
        functools.partial(_mm_kernel, act=act),
        grid=(n // tm, nout // tn),
        in_specs=[pl.BlockSpec((tm, k), lambda i, j: (i, 0)), w.spec(tn)],
        out_specs=pl.BlockSpec((tm, tn), lambda i, j: (i, j)),
        out_shape=jax.ShapeDtypeStruct((n, nout), out_dtype),
        compiler_params=_params("parallel", "arbitrary"),
        name="matmul_" + (act or "plain"),
    )(a, w.arr)


def _dual_mm_kernel(a_ref, w1_ref, w2_ref, o_ref, *, act):
    a = a_ref[...]
    p = jnp.dot(a, w1_ref[...].astype(BF16), preferred_element_type=F32)
    q = jnp.dot(a, w2_ref[...].astype(BF16), preferred_element_type=F32)
    if act == "swiglu":
        r = p * jax.nn.sigmoid(p) * q
    else:
        r = p * jax.nn.sigmoid(q)
    o_ref[...] = r.astype(o_ref.dtype)


def dual_matmul(a, w1, w2, out_dtype, *, act, tm=1024, tn=256):
    n, k = a.shape
    assert w1.shape() == w2.shape() and w1.shape()[0] == k
    nout = w1.shape()[1]
    tm = _row_tile(n, tm)
    tn = _row_tile(nout, tn)
    return pl.pallas_call(
        functools.partial(_dual_mm_kernel, act=act),
        grid=(n // tm, nout // tn),
        in_specs=[pl.BlockSpec((tm, k), lambda i, j: (i, 0)), w1.spec(tn), w2.spec(tn)],
        out_specs=pl.BlockSpec((tm, tn), lambda i, j: (i, j)),
        out_shape=jax.ShapeDtypeStruct((n, nout), out_dtype),
        compiler_params=_params("parallel", "arbitrary"),
        name="dual_matmul_" + act,
    )(a, w1.arr, w2.arr)


def _mm_res_kernel(*refs, npairs, scale):
    res_ref = refs[2 * npairs]
    o_ref = refs[2 * npairs + 1]
    acc = jnp.dot(refs[0][...], refs[1][...].astype(BF16), preferred_element_type=F32)
    for p in range(1, npairs):
        acc = acc + jnp.dot(refs[2 * p][...], refs[2 * p + 1][...].astype(BF16),
                            preferred_element_type=F32)
    o_ref[...] = res_ref[...] + scale * acc


def matmul_residual(pairs, res, scale, *, tm, tn):
    n, nout = res.shape
    tm = _row_tile(n, tm)
    tn = _row_tile(nout, tn)
    in_specs, args = [], []
    for a, w in pairs:
        k = a.shape[1]
        assert w.shape() == (k, nout)
        in_specs += [pl.BlockSpec((tm, k), lambda i, j: (i, 0)), w.spec(tn)]
        args += [a, w.arr]
    in_specs.append(pl.BlockSpec((tm, tn), lambda i, j: (i, j)))
    args.append(res)
    return pl.pallas_call(
        functools.partial(_mm_res_kernel, npairs=len(pairs), scale=scale),
        grid=(n // tm, nout // tn),
        in_specs=in_specs,
        out_specs=pl.BlockSpec((tm, tn), lambda i, j: (i, j)),
        out_shape=jax.ShapeDtypeStruct((n, nout), F32),
        compiler_params=_params("parallel", "arbitrary"),
        name="matmul_residual",
    )(*args)


def _spatial_gate_kernel(u_ref, v_ref, g_ref, b_ref, wsp_ref, bspt_ref, o_ref, *, groups, rows):
    v = v_ref[...]
    mu = jnp.mean(v, axis=-1, keepdims=True)
    vc = v - mu
    var = jnp.mean(vc * vc, axis=-1, keepdims=True)
    vn = (vc * lax.rsqrt(var + EPS) * g_ref[...] + b_ref[...]).astype(BF16)
    gw = vn.shape[1] // groups
    ri = lax.broadcasted_iota(jnp.int32, (CHUNK, CHUNK), 0)
    ci = lax.broadcasted_iota(jnp.int32, (CHUNK, CHUNK), 1)
    tril = ci <= ri
    bspt = bspt_ref[...]
    for g in range(groups):
        wm = jnp.where(tril, wsp_ref[g], 0.0).astype(BF16)
        bias = bspt[:, g:g + 1]
        for c in range(rows // CHUNK):
            rs = slice(c * CHUNK, (c + 1) * CHUNK)
            cs = slice(g * gw, (g + 1) * gw)
            s = jnp.dot(wm, vn[rs, cs], preferred_element_type=F32) + bias
            o_ref[rs, cs] = (u_ref[rs, cs] * s).astype(o_ref.dtype)


def spatial_gate(uv, ln_g, ln_b, w_sp, b_sp, *, rows=512):
    n = uv.shape[0]
    w = uv.shape[1] // 2
    groups = w_sp.shape[0]
    rows = _row_tile(n, rows)
    return pl.pallas_call(
        functools.partial(_spatial_gate_kernel, groups=groups, rows=rows),
        grid=(n // rows,),
        in_specs=[pl.BlockSpec((rows, w), lambda i: (i, 0)),
                  pl.BlockSpec((rows, w), lambda i: (i, 1)),
                  pl.BlockSpec((1, w), lambda i: (0, 0)),
                  pl.BlockSpec((1, w), lambda i: (0, 0)),
                  pl.BlockSpec((groups, CHUNK, CHUNK), lambda i: (0, 0, 0)),
                  pl.BlockSpec((CHUNK, groups), lambda i: (0, 0))],
        out_specs=pl.BlockSpec((rows, w), lambda i: (i, 0)),
        out_shape=jax.ShapeDtypeStruct((n, w), BF16),
        compiler_params=_params("parallel"),
        name="spatial_gate",
    )(uv, uv, ln_g.reshape(1, w), ln_b.reshape(1, w), w_sp, b_sp.T)


HALO = 32


def _conv_kernel(cur_ref, halo_ref, w_ref, b_ref, o_ref, xpad_ref, *, tt, rb):
    t = pl.program_id(1)
    halo = halo_ref[0]
    xpad_ref[0:HALO, :] = jnp.where(t == 0, jnp.zeros_like(halo), halo)
    xpad_ref[HALO:HALO + tt, :] = cur_ref[0]
    bias = b_ref[...]
    base = HALO - (CONV_WIDTH - 1)
    for r in range(tt // rb):
        acc = jnp.zeros((rb, bias.shape[1]), F32) + bias
        for k in range(CONV_WIDTH):
            acc = acc + xpad_ref[pl.ds(r * rb + base + k, rb), :] * w_ref[pl.ds(k, 1), :]
        o_ref[0, r * rb:(r + 1) * rb, :] = acc


def causal_conv(x, w, b, *, tt=512, cb=256, rb=64):
    bsz, t, c = x.shape
    tt = _row_tile(t, tt)
    return pl.pallas_call(
        functools.partial(_conv_kernel, tt=tt, rb=rb),
        grid=(bsz, t // tt, c // cb),
        in_specs=[pl.BlockSpec((1, tt, cb), lambda bi, ti, ci: (bi, ti, ci)),
                  pl.BlockSpec((1, HALO, cb),
                               lambda bi, ti, ci: (bi, jnp.maximum(ti * (tt // HALO) - 1, 0), ci)),
                  pl.BlockSpec((CONV_WIDTH, cb), lambda bi, ti, ci: (0, ci)),
                  pl.BlockSpec((1, cb), lambda bi, ti, ci: (0, ci))],
        out_specs=pl.BlockSpec((1, tt, cb), lambda bi, ti, ci: (bi, ti, ci)),
        out_shape=jax.ShapeDtypeStruct((bsz, t, c), F32),
        scratch_shapes=[pltpu.VMEM((HALO + tt, cb), F32)],
        compiler_params=_params("parallel", "parallel", "parallel"),
        name="causal_conv",
    )(x, x, w, b.reshape(1, c))


PHALO = 16


def _pool_kernel(cur_ref, halo_ref, w_ref, sc_ref, o_ref, xpad_ref, *, tt, gw):
    t = pl.program_id(1)
    halo = halo_ref[0]
    xpad_ref[0:PHALO, :] = jnp.where(t == 0, jnp.zeros_like(halo), halo)
    xpad_ref[PHALO:PHALO + tt, :] = cur_ref[0]
    pos = t * tt + lax.broadcasted_iota(jnp.int32, (tt, 1), 0)
    for g, win in enumerate(POOL_WINDOWS):
        cs = slice(g * gw, (g + 1) * gw)
        p = xpad_ref[PHALO:PHALO + tt, cs]
        s = p
        for j in range(1, win):
            s = s + xpad_ref[PHALO - j:PHALO - j + tt, cs]
        cnt = jnp.minimum(pos + 1, win).astype(F32)
        d = (s / cnt - p).astype(BF16)
        y = jnp.dot(d, w_ref[g], preferred_element_type=F32) * sc_ref[:, cs]
        o_ref[0, :, cs] = y.astype(o_ref.dtype)


def multiscale_pool(z, w_pool, scale, *, width, tt=256):
    bsz, t, _ = z.shape
    groups = len(POOL_WINDOWS)
    gw = width // groups
    tt = _row_tile(t, tt)
    return pl.pallas_call(
        functools.partial(_pool_kernel, tt=tt, gw=gw),
        grid=(bsz, t // tt),
        in_specs=[pl.BlockSpec((1, tt, width), lambda bi, ti: (bi, ti, 0)),
                  pl.BlockSpec((1, PHALO, width),
                               lambda bi, ti: (bi, jnp.maximum(ti * (tt // PHALO) - 1, 0), 0)),
                  pl.BlockSpec((groups, gw, gw), lambda bi, ti: (0, 0, 0)),
                  pl.BlockSpec((1, width), lambda bi, ti: (0, 0))],
        out_specs=pl.BlockSpec((1, tt, width), lambda bi, ti: (bi, ti, 0)),
        out_shape=jax.ShapeDtypeStruct((bsz, t, width), BF16),
        scratch_shapes=[pltpu.VMEM((PHALO + tt, width), F32)],
        compiler_params=_params("parallel", "parallel"),
        name="multiscale_pool",
    )(z, z, w_pool, scale.reshape(1, width))


def _select_kernel(q_ref, w_ref, ke_ref, ko_ref, o_ref, key_ref, *, tq, ts, nkb, npair, topk, wscale):
    qi = pl.program_id(1)
    t0 = qi * tq
    nsb = (t0 + tq + ts - 1) // ts
    w = w_ref[0] * wscale
    trow = t0 + lax.broadcasted_iota(jnp.int32, (tq, ts), 0)
    scol = lax.broadcasted_iota(jnp.int32, (tq, ts), 1)

    def score_body(kb, c):
        s0 = pl.multiple_of(kb * ts, ts)
        ke = ke_ref[0, pl.ds(s0, ts), :]
        ko = ko_ref[0, pl.ds(s0, ts), :]
        acc = jnp.zeros((tq, ts), F32)
        for j in range(npair):
            slab = q_ref[0, :, j * 128:(j + 1) * 128]
            le = lax.dot_general(slab, ke, _NT, preferred_element_type=F32)
            lo = lax.dot_general(slab, ko, _NT, preferred_element_type=F32)
            acc = acc + jnp.maximum(le, 0.0) * w[:, 2 * j:2 * j + 1]
            acc = acc + jnp.maximum(lo, 0.0) * w[:, 2 * j + 1:2 * j + 2]
        acc = jnp.where(s0 + scol <= trow, acc, -jnp.inf)
        bits = pltpu.bitcast(acc, jnp.int32)
        key_ref[kb] = jnp.where(bits < 0, bits ^ jnp.int32(0x7FFFFFFF), bits)
        return c

    lax.fori_loop(0, nsb, score_body, 0)

    def bit_body(it, tu):
        bit = jnp.left_shift(jnp.int32(1), 31 - it)
        cand_u = tu | bit
        cand = jnp.concatenate([cand_u ^ jnp.int32(INT_MIN)] * (ts // 128), axis=1)

        def count_body(kb, cnt):
            ge = jnp.where(key_ref[kb] >= cand, 1.0, 0.0)
            for l in range(ts // 128):
                cnt = cnt + ge[:, l * 128:(l + 1) * 128]
            return cnt

        cnt = lax.fori_loop(0, nsb, count_body, jnp.zeros((tq, 128), F32))
        total = jnp.sum(cnt, axis=1, keepdims=True)
        return jnp.where(total >= float(topk), cand_u, tu)

    tu = lax.fori_loop(0, 32, bit_body, jnp.zeros((tq, 128), jnp.int32))
    thr = jnp.concatenate([tu ^ jnp.int32(INT_MIN)] * (ts // 128), axis=1)

    for kb in range(nkb):
        cols = slice(kb * ts, (kb + 1) * ts)

        @pl.when(kb < nsb)
        def _():
            sel = (key_ref[kb] >= thr) & (kb * ts + scol <= trow)
            o_ref[0, :, cols] = jnp.where(sel, 0.0, NEG).astype(o_ref.dtype)

        @pl.when(kb >= nsb)
        def _():
            o_ref[0, :, cols] = jnp.full((tq, ts), NEG, o_ref.dtype)


def select_mask(q_idx, w_idx, k_even, k_odd, *, n_heads, topk, tq=QUERY_TILE, ts=FAR_BLOCK):
    bsz, t, hd = q_idx.shape
    ts = _row_tile(t, ts)
    nkb = t // ts
    npair = hd // 128
    wscale = float(n_heads ** -0.5 * (hd // n_heads) ** -0.5)
    return pl.pallas_call(
        functools.partial(_select_kernel, tq=tq, ts=ts, nkb=nkb, npair=npair, topk=topk, wscale=wscale),
        grid=(bsz, t // tq),
        in_specs=[pl.BlockSpec((1, tq, hd), lambda b, i: (b, i, 0)),
                  pl.BlockSpec((1, tq, 128), lambda b, i: (b, i, 0)),
                  pl.BlockSpec((1, t, 128), lambda b, i: (b, 0, 0)),
                  pl.BlockSpec((1, t, 128), lambda b, i: (b, 0, 0))],
        out_specs=pl.BlockSpec((1, tq, t), lambda b, i: (b, i, 0)),
        out_shape=jax.ShapeDtypeStruct((bsz, t, t), BF16),
        scratch_shapes=[pltpu.VMEM((nkb, tq, ts), jnp.int32)],
        compiler_params=_params("parallel", "arbitrary"),
        name="select_mask",
    )(q_idx, w_idx, k_even, k_odd)


def _t5_bucket_table(n_max):
    n = np.arange(n_max)
    max_exact = REL_BUCKETS // 2
    nf = np.maximum(n, 1).astype(np.float32)
    ratio = np.log(nf / np.float32(max_exact)) / np.float32(math.log(REL_MAX_DIST / max_exact))
    large = max_exact + (ratio * np.float32(REL_BUCKETS - max_exact)).astype(np.int32)
    large = np.minimum(large, REL_BUCKETS - 1)
    return np.where(n < max_exact, n, large).astype(np.int32)


def _bias_kernel(rb_ref, o_ref, *, tq, thresholds):
    i = lax.broadcasted_iota(jnp.int32, (tq, 2 * tq), 0)
    j = lax.broadcasted_iota(jnp.int32, (tq, 2 * tq), 1)
    nh = o_ref.shape[1]
    for v in range(2):
        d = jnp.maximum(v * tq + i - j, 0)
        bucket = jnp.zeros_like(d)
        for th in thresholds:
            bucket = bucket + jnp.where(d >= th, 1, 0)
        for h in range(nh):
            val = jnp.zeros((tq, 2 * tq), F32)
            for b in range(REL_BUCKETS):
                val = jnp.where(bucket == b, rb_ref[b, h] - rb_ref[REL_BUCKETS - 1, h], val)
            o_ref[v, h] = val


def bias_windows(rel_bias, *, tq=QUERY_TILE):
    table = _t5_bucket_table(4 * tq)
    assert np.all(np.diff(table) >= 0) and np.all(np.diff(table) <= 1)
    assert np.all(table[tq:] == REL_BUCKETS - 1), "bias must be constant beyond the near window"
    thresholds = tuple(int(np.argmax(table >= b)) for b in range(1, REL_BUCKETS))
    nh = rel_bias.shape[1]
    return pl.pallas_call(
        functools.partial(_bias_kernel, tq=tq, thresholds=thresholds),
        in_specs=[pl.BlockSpec(memory_space=pltpu.SMEM)],
        out_specs=pl.BlockSpec(memory_space=pltpu.VMEM),
        out_shape=jax.ShapeDtypeStruct((2, nh, tq, 2 * tq), F32),
        compiler_params=pltpu.CompilerParams(vmem_limit_bytes=VMEM_LIMIT),
        name="bias_windows",
    )(rel_bias)


def _attn_kernel(q_ref, madd_ref, ckv_ref, wuk_ref, wuv_ref, btab_ref, o_ref,
                 qlat_ref, m_ref, l_ref, acc_ref, s0_ref, s1_ref, p0_ref, p1_ref, a0_ref, a1_ref,
                 *, tq, nh, dh, fb, scale):
    qi = pl.program_id(1)
    t0 = qi * tq
    for h in range(nh):
        qh = q_ref[0, :, h * dh:(h + 1) * dh]
        qlat_ref[h * tq:(h + 1) * tq, :] = jnp.dot(
            qh, wuk_ref[h], preferred_element_type=F32).astype(BF16)
    m_ref[...] = jnp.full(m_ref.shape, NEG, F32)
    l_ref[...] = jnp.zeros(l_ref.shape, F32)
    acc_ref[...] = jnp.zeros(acc_ref.shape, F32)
    rep = acc_ref.shape[1] // 128
    nw = 2 * tq
    s_ref, p_ref, alpha_ref = (s0_ref, s1_ref), (p0_ref, p1_ref), (a0_ref, a1_ref)

    def qk(kv, slot):
        s_ref[slot][:, :kv.shape[0]] = lax.dot_general(
            qlat_ref[...], kv, _NT, preferred_element_type=F32)

    def softmax(slot, n, ma, bias_of_head):
        for h in range(nh):
            rows = slice(h * tq, (h + 1) * tq)
            lg = s_ref[slot][rows, :n] * scale + ma
            if bias_of_head is not None:
                lg = lg + bias_of_head(h)
            m_old = m_ref[rows, :1]
            m_new = jnp.maximum(m_old, jnp.max(lg, axis=1, keepdims=True))
            alpha = jnp.exp(m_old - m_new)
            p = jnp.exp(lg - m_new)
            l_new = alpha * l_ref[rows, :1] + jnp.sum(p, axis=1, keepdims=True)
            p_ref[slot][rows, :n] = p.astype(BF16)
            alpha_ref[slot][rows, :] = jnp.broadcast_to(alpha, (tq, 128))
            m_ref[rows, :] = jnp.broadcast_to(m_new, (tq, 128))
            l_ref[rows, :] = jnp.broadcast_to(l_new, (tq, 128))

    def pv(kv, slot):
        n = kv.shape[0]
        upd = jnp.dot(p_ref[slot][:, :n], kv, preferred_element_type=F32)
        acc_ref[...] = acc_ref[...] * jnp.concatenate([alpha_ref[slot][...]] * rep, axis=1) + upd

    lim = t0 - tq
    nfar = (jnp.maximum(lim, 0) + fb - 1) // fb
    scol = lax.broadcasted_iota(jnp.int32, (tq, fb), 1)

    def far_kv(kb):
        return ckv_ref[0, pl.ds(pl.multiple_of(kb * fb, fb), fb), :]

    w0 = pl.multiple_of(jnp.maximum(lim, 0), tq)
    variant = jnp.minimum(qi, 1)
    kv_near = ckv_ref[0, pl.ds(w0, nw), :]
    qk(kv_near, 1)
    softmax(1, nw, madd_ref[0, :, pl.ds(w0, nw)].astype(F32), lambda h: btab_ref[variant, h])
    pv(kv_near, 1)
    qk(far_kv(0), 0)
    p_ref[1][...] = jnp.zeros(p_ref[1].shape, BF16)
    alpha_ref[1][...] = jnp.ones(alpha_ref[1].shape, F32)

    def far_stages(kb, slot):
        s0 = pl.multiple_of(kb * fb, fb)
        qk(far_kv(jnp.minimum(kb + 1, nfar - 1)), 1 - slot)
        ma = madd_ref[0, :, pl.ds(s0, fb)].astype(F32)
        softmax(slot, fb, jnp.where(s0 + scol < lim, ma, NEG), None)
        pv(far_kv(jnp.maximum(kb - 1, 0)), 1 - slot)

    def far_body(kb, c):
        for slot in range(2):
            pl.when(kb % 2 == slot)(functools.partial(far_stages, kb, slot))
        return c

    lax.fori_loop(0, nfar, far_body, 0)
    last = jnp.maximum(nfar - 1, 0)
    last_slot = jnp.where(nfar > 0, last % 2, 1)
    for slot in range(2):
        pl.when(last_slot == slot)(functools.partial(pv, far_kv(last), slot))

    for h in range(nh):
        rows = slice(h * tq, (h + 1) * tq)
        o = acc_ref[rows, :] / l_ref[rows, :1]
        y = jnp.dot(o.astype(BF16), wuv_ref[h], preferred_element_type=F32)
        o_ref[0, :, h * dh:(h + 1) * dh] = y.astype(o_ref.dtype)


def masked_latent_attention(q, madd, c_kv, w_uk, w_uv, btab, *, tq=QUERY_TILE, fb=FAR_BLOCK):
    bsz, t, hd = q.shape
    nh, dh, r = w_uk.shape
    fb = _row_tile(t, fb)
    assert t >= fb >= 2 * tq and t % tq == 0
    once = pl.Buffered(1)
    return pl.pallas_call(
        functools.partial(_attn_kernel, tq=tq, nh=nh, dh=dh, fb=fb, scale=float(dh ** -0.5)),
        grid=(bsz, t // tq),
        in_specs=[pl.BlockSpec((1, tq, hd), lambda b, i: (b, i, 0)),
                  pl.BlockSpec((1, tq, t), lambda b, i: (b, i, 0)),
                  pl.BlockSpec((1, t, r), lambda b, i: (b, 0, 0), pipeline_mode=once),
                  pl.BlockSpec((nh, dh, r), lambda b, i: (0, 0, 0), pipeline_mode=once),
                  pl.BlockSpec((nh, r, dh), lambda b, i: (0, 0, 0), pipeline_mode=once),
                  pl.BlockSpec((2, nh, tq, 2 * tq), lambda b, i: (0, 0, 0, 0), pipeline_mode=once)],
        out_specs=pl.BlockSpec((1, tq, hd), lambda b, i: (b, i, 0)),
        out_shape=jax.ShapeDtypeStruct((bsz, t, hd), BF16),
        scratch_shapes=[pltpu.VMEM((nh * tq, r), BF16),
                        pltpu.VMEM((nh * tq, 128), F32),
                        pltpu.VMEM((nh * tq, 128), F32),
                        pltpu.VMEM((nh * tq, r), F32),
                        pltpu.VMEM((nh * tq, fb), F32),
                        pltpu.VMEM((nh * tq, fb), F32),
                        pltpu.VMEM((nh * tq, fb), BF16),
                        pltpu.VMEM((nh * tq, fb), BF16),
                        pltpu.VMEM((nh * tq, 128), F32),
                        pltpu.VMEM((nh * tq, 128), F32)],
        compiler_params=_params("parallel", "arbitrary"),
        name="masked_latent_attention",
    )(q, madd, c_kv, w_uk, w_uv, btab)


def _ffn(x, g, w_in, w_out, lead):
    dff = w_out.shape[-2]
    h = rms_norm(x, g, BF16)
    a = dual_matmul(h, WSlab(w_in, lead, cols=dff), WSlab(w_in, lead, col0=dff, cols=dff), BF16,
                    act="swiglu", tm=1024, tn=256)
    return matmul_residual([(a, WSlab(w_out, lead))], x, 0.5, tm=512, tn=512)


def _mixer_ab(x, h, bsz, t, i, w_in, ln_a_g, ln_a_b, w_sp, b_sp, conv_w, conv_b, ln_b_g, ln_b_b, w_out):
    wa = ln_a_g.shape[0]
    wb = ln_b_g.shape[0]
    uv = matmul(h, WSlab(w_in, (i,), cols=2 * wa), F32, act="gelu", tm=1024, tn=512)
    y_a = spatial_gate(uv, ln_a_g, ln_a_b, w_sp, b_sp)
    glu = dual_matmul(h, WSlab(w_in, (i,), col0=2 * wa, cols=wb),
                      WSlab(w_in, (i,), col0=2 * wa + wb, cols=wb), F32, act="glu", tm=1024, tn=256)
    conv = causal_conv(glu.reshape(bsz, t, wb), conv_w, conv_b).reshape(bsz * t, wb)
    y_b = ln_silu(conv, ln_b_g, ln_b_b, BF16)
    return matmul_residual([(y_a, WSlab(w_out, (i,), rows=wa)),
                            (y_b, WSlab(w_out, (i,), row0=wa, rows=wb))], x, 1.0, tm=1024, tn=512)


def _mixer_cd(x, h, bsz, t, i, w_in, w_pool, pool_scale, g_cq, w_uq, w_qidx, g_ckv, w_uk, w_uv,
              rel_bias, w_out):
    n = bsz * t
    wc = pool_scale.shape[0]
    qr = g_cq.shape[0]
    kvr = g_ckv.shape[0]
    n_heads, dh, _ = w_uk.shape
    idx_dim = 64
    idx_heads = w_qidx.shape[-1] // idx_dim
    o3 = wc + qr + kvr
    assert wc % qr == 0 and (wc + qr) % kvr == 0 and idx_heads <= 128 and 2 * idx_dim == 128

    z = matmul(h, WSlab(w_in, (i,), cols=o3), F32, tm=1024, tn=512)
    wk = w_in[i, :, o3:o3 + idx_dim]
    ww = w_in[i, :, o3 + idx_dim:]
    zk = jnp.zeros_like(wk)
    w_small = jnp.concatenate(
        [wk, zk, zk, wk, ww, jnp.zeros((wk.shape[0], 128 - idx_heads), wk.dtype)], axis=1)
    zs = matmul(h, WSlab(w_small), F32, tm=1024, tn=384)
    k_even = zs[:, :128].astype(BF16).reshape(bsz, t, 128)
    k_odd = zs[:, 128:256].astype(BF16).reshape(bsz, t, 128)
    w_idx = zs[:, 256:].reshape(bsz, t, 128)

    y_c = multiscale_pool(z.reshape(bsz, t, o3), w_pool.astype(BF16), pool_scale, width=wc)
    c_q = rms_norm(z, g_cq, BF16, col_block=wc // qr, width=qr)
    c_kv = rms_norm(z, g_ckv, BF16, col_block=(wc + qr) // kvr, width=kvr)
    q = matmul(c_q, WSlab(w_uq, (i,)), BF16, tm=1024, tn=512)
    q_idx = matmul(c_q, WSlab(w_qidx, (i,)), BF16, tm=1024, tn=512)

    topk = min(TOPK_MAX, t // 4)
    madd = select_mask(q_idx.reshape(bsz, t, -1), w_idx, k_even, k_odd, n_heads=idx_heads, topk=topk)
    btab = bias_windows(rel_bias)
    y_d = masked_latent_attention(q.reshape(bsz, t, -1), madd, c_kv.reshape(bsz, t, kvr),
                                  w_uk.astype(BF16), w_uv.astype(BF16), btab)
    return matmul_residual([(y_c.reshape(n, wc), WSlab(w_out, (i,), rows=wc)),
                            (y_d.reshape(n, -1), WSlab(w_out, (i,), row0=wc, rows=n_heads * dh))],
                           x, 1.0, tm=1024, tn=512)


def kernel(x, g_ff, w_ff_in, w_ff_out, g_mix, w_in_ab, ln_a_g, ln_a_b, w_sp, b_sp, conv_w, conv_b,
           ln_b_g, ln_b_b, w_out_ab, w_in_cd, w_pool, pool_scale, g_cq, w_uq, w_qidx, g_ckv, w_uk,
           w_uv, rel_bias, w_out_cd, g_final):
    bsz, t, d = x.shape
    depth = g_ff.shape[0]
    xs = x.reshape(bsz * t, d)
    w_ff_out = w_ff_out.astype(BF16)
    for l in range(depth):
        xs = _ffn(xs, g_ff[l, 0], w_ff_in, w_ff_out, (l, 0))
        h = rms_norm(xs, g_mix[l], BF16)
        i = l // 2
        if l % 2 == 0:
            xs = _mixer_ab(xs, h, bsz, t, i, w_in_ab, ln_a_g[i], ln_a_b[i], w_sp[i], b_sp[i],
                           conv_w[i], conv_b[i], ln_b_g[i], ln_b_b[i], w_out_ab)
        else:
            xs = _mixer_cd(xs, h, bsz, t, i, w_in_cd, w_pool[i], pool_scale[i], g_cq[i], w_uq,
                           w_qidx, g_ckv[i], w_uk[i], w_uv[i], rel_bias, w_out_cd)
        xs = _ffn(xs, g_ff[l, 1], w_ff_in, w_ff_out, (l, 1))
    return rms_norm(xs, g_final, x.dtype).reshape(bsz, t, d)
```

```python
import functools
import math
from typing import NamedTuple, Optional

import numpy as np
import jax
import jax.numpy as jnp
from jax import lax
from jax.experimental import pallas as pl
from jax.experimental.pallas import tpu as pltpu

F32 = jnp.float32
BF16 = jnp.bfloat16

EPS = 1e-6
CHUNK = 128
CONV_WIDTH = 31
POOL_WINDOWS = (2, 4, 8, 16)
TOPK_MAX = 256
REL_BUCKETS = 32
REL_MAX_DIST = 128
QUERY_TILE = 128
FAR_BLOCK = 512
NEG = -1e30
INT_MIN = -2147483648

V7X_VMEM_BYTES = 64 * 1024 * 1024
VMEM_LIMIT = V7X_VMEM_BYTES - 8 * 1024 * 1024

_NT = (((1,), (1,)), ((), ()))


def _params(*sem):
    return pltpu.CompilerParams(dimension_semantics=sem, vmem_limit_bytes=VMEM_LIMIT)


def _row_tile(n, want):
    t = min(n, want)
    assert n % t == 0, (n, t)
    return t


def _rms_kernel(x_ref, g_ref, o_ref):
    x = x_ref[...].astype(F32)
    y = x * lax.rsqrt(jnp.mean(x * x, axis=-1, keepdims=True) + EPS)
    o_ref[...] = (y * g_ref[...]).astype(o_ref.dtype)


def rms_norm(x, g, out_dtype, *, col_block=0, width=None, tm=256):
    n = x.shape[0]
    width = x.shape[1] if width is None else width
    tm = _row_tile(n, tm)
    return pl.pallas_call(
        _rms_kernel,
        grid=(n // tm,),
        in_specs=[pl.BlockSpec((tm, width), lambda i: (i, col_block)),
                  pl.BlockSpec((1, width), lambda i: (0, 0))],
        out_specs=pl.BlockSpec((tm, width), lambda i: (i, 0)),
        out_shape=jax.ShapeDtypeStruct((n, width), out_dtype),
        compiler_params=_params("parallel"),
        name="rms_norm",
    )(x, g.reshape(1, width).astype(F32))


def _ln_silu_kernel(x_ref, g_ref, b_ref, o_ref):
    x = x_ref[...]
    mu = jnp.mean(x, axis=-1, keepdims=True)
    xc = x - mu
    var = jnp.mean(xc * xc, axis=-1, keepdims=True)
    y = xc * lax.rsqrt(var + EPS) * g_ref[...] + b_ref[...]
    o_ref[...] = (y * jax.nn.sigmoid(y)).astype(o_ref.dtype)


def ln_silu(x, g, b, out_dtype, *, tm=256):
    n, d = x.shape
    tm = _row_tile(n, tm)
    return pl.pallas_call(
        _ln_silu_kernel,
        grid=(n // tm,),
        in_specs=[pl.BlockSpec((tm, d), lambda i: (i, 0)),
                  pl.BlockSpec((1, d), lambda i: (0, 0)),
                  pl.BlockSpec((1, d), lambda i: (0, 0))],
        out_specs=pl.BlockSpec((tm, d), lambda i: (i, 0)),
        out_shape=jax.ShapeDtypeStruct((n, d), out_dtype),
        compiler_params=_params("parallel"),
        name="ln_silu",
    )(x, g.reshape(1, d), b.reshape(1, d))


class WSlab(NamedTuple):
    arr: jax.Array
    lead: tuple = ()
    row0: int = 0
    rows: Optional[int] = None
    col0: int = 0
    cols: Optional[int] = None

    def shape(self):
        r = self.arr.shape[-2] if self.rows is None else self.rows
        c = self.arr.shape[-1] if self.cols is None else self.cols
        return r, c

    def spec(self, tn):
        r, _ = self.shape()
        assert self.row0 % r == 0 and self.col0 % tn == 0, (self.row0, r, self.col0, tn)
        lead, rb, cb = self.lead, self.row0 // r, self.col0 // tn
        return pl.BlockSpec((None,) * len(lead) + (r, tn), lambda i, j: lead + (rb, cb + j))


class Act(NamedTuple):
    a: jax.Array
    ssq: Optional[jax.Array] = None

    def specs(self, tm, single_buffer=False):
        k = self.a.shape[1]
        mode = dict(pipeline_mode=pl.Buffered(1)) if single_buffer else {}
        sp = [pl.BlockSpec((tm, k), lambda i, j: (i, 0), **mode)]
        if self.ssq is not None:
            sp.append(pl.BlockSpec((tm, 128), lambda i, j: (i, 0)))
        return sp

    def args(self):
        return [self.a] if self.ssq is None else [self.a, self.ssq]


def _row_factor(ssq_ref, k):
    return lax.rsqrt(ssq_ref[:, :1] * (1.0 / k) + EPS)


def _mm_kernel(*refs, act, normed):
    a_ref, w_ref, o_ref = refs[0], refs[-2], refs[-1]
    r = jnp.dot(a_ref[...], w_ref[...].astype(BF16), preferred_element_type=F32)
    if normed:
        r = r * _row_factor(refs[1], a_ref.shape[1])
    if act == "gelu":
        r = 0.5 * r * (1.0 + lax.erf(r * math.sqrt(0.5)))
    o_ref[...] = r.astype(o_ref.dtype)


def matmul(x, w, out_dtype, *, act=None, tm=1024, tn=512):
    n, k = x.a.shape
    assert w.shape()[0] == k
    nout = w.shape()[1]
    tm = _row_tile(n, tm)
    tn = _row_tile(nout, tn)
    return pl.pallas_call(
        functools.partial(_mm_kernel, act=act, normed=x.ssq is not None),
        grid=(n // tm, nout // tn),
        in_specs=x.specs(tm) + [w.spec(tn)],
        out_specs=pl.BlockSpec((tm, tn), lambda i, j: (i, j)),
        out_shape=jax.ShapeDtypeStruct((n, nout), out_dtype),
        compiler_params=_params("parallel", "arbitrary"),
        name="matmul_" + (act or "plain"),
    )(*x.args(), w.arr)


def _dual_mm_kernel(*refs, act, normed):
    a_ref, w1_ref, w2_ref, o_ref = refs[0], refs[-3], refs[-2], refs[-1]
    a = a_ref[...]
    p = jnp.dot(a, w1_ref[...].astype(BF16), preferred_element_type=F32)
    q = jnp.dot(a, w2_ref[...].astype(BF16), preferred_element_type=F32)
    if normed:
        rf = _row_factor(refs[1], a_ref.shape[1])
        p, q = p * rf, q * rf
    if act == "swiglu":
        r = p * jax.nn.sigmoid(p) * q
    else:
        r = p * jax.nn.sigmoid(q)
    o_ref[...] = r.astype(o_ref.dtype)


def dual_matmul(x, w1, w2, out_dtype, *, act, tm=1024, tn=256, single_buffer_a=False):
    n, k = x.a.shape
    assert w1.shape() == w2.shape() and w1.shape()[0] == k
    nout = w1.shape()[1]
    tm = _row_tile(n, tm)
    tn = _row_tile(nout, tn)
    return pl.pallas_call(
        functools.partial(_dual_mm_kernel, act=act, normed=x.ssq is not None),
        grid=(n // tm, nout // tn),
        in_specs=x.specs(tm, single_buffer_a) + [w1.spec(tn), w2.spec(tn)],
        out_specs=pl.BlockSpec((tm, tn), lambda i, j: (i, j)),
        out_shape=jax.ShapeDtypeStruct((n, nout), out_dtype),
        compiler_params=_params("parallel", "arbitrary"),
        name="dual_matmul_" + act,
    )(*x.args(), w1.arr, w2.arr)


def _mm_res_kernel(*refs, npairs, scale, emit_norm):
    res_ref = refs[2 * npairs]
    acc = jnp.dot(refs[0][...], refs[1][...].astype(BF16), preferred_element_type=F32)
    for p in range(1, npairs):
        acc = acc + jnp.dot(refs[2 * p][...], refs[2 * p + 1][...].astype(BF16),
                            preferred_element_type=F32)
    y = res_ref[...] + scale * acc
    if not emit_norm:
        refs[2 * npairs + 1][...] = y
        return
    g_ref, o_ref, xg_ref, ssq_ref = refs[2 * npairs + 1:]
    o_ref[...] = y
    xg_ref[...] = (y * g_ref[...]).astype(xg_ref.dtype)
    part = jnp.broadcast_to(jnp.sum(y * y, axis=1, keepdims=True), ssq_ref.shape)
    j = pl.program_id(1)

    @pl.when(j == 0)
    def _():
        ssq_ref[...] = part

    @pl.when(j > 0)
    def _():
        ssq_ref[...] += part


def matmul_residual(pairs, res, scale, *, tm, tn, next_gain=None):
    n, nout = res.shape
    tm = _row_tile(n, tm)
    tn = _row_tile(nout, tn)
    in_specs, args = [], []
    for a, w in pairs:
        k = a.shape[1]
        assert w.shape() == (k, nout)
        in_specs += [pl.BlockSpec((tm, k), lambda i, j: (i, 0)), w.spec(tn)]
        args += [a, w.arr]
    in_specs.append(pl.BlockSpec((tm, tn), lambda i, j: (i, j)))
    args.append(res)
    out_specs = pl.BlockSpec((tm, tn), lambda i, j: (i, j))
    out_shape = jax.ShapeDtypeStruct((n, nout), F32)
    if next_gain is not None:
        in_specs.append(pl.BlockSpec((1, tn), lambda i, j: (0, j)))
        args.append(next_gain.reshape(1, nout).astype(F32))
        out_specs = [out_specs, pl.BlockSpec((tm, tn), lambda i, j: (i, j)),
                     pl.BlockSpec((tm, 128), lambda i, j: (i, 0))]
        out_shape = [out_shape, jax.ShapeDtypeStruct((n, nout), BF16),
                     jax.ShapeDtypeStruct((n, 128), F32)]
    out = pl.pallas_call(
        functools.partial(_mm_res_kernel, npairs=len(pairs), scale=scale,
                          emit_norm=next_gain is not None),
        grid=(n // tm, nout // tn),
        in_specs=in_specs,
        out_specs=out_specs,
        out_shape=out_shape,
        compiler_params=_params("parallel", "arbitrary"),
        name="matmul_residual",
    )(*args)
    if next_gain is None:
        return out, None
    return out[0], Act(out[1], out[2])


def _spatial_gate_kernel(u_ref, v_ref, g_ref, b_ref, wsp_ref, bspt_ref, o_ref, *, groups, rows):
    v = v_ref[...]
    mu = jnp.mean(v, axis=-1, keepdims=True)
    vc = v - mu
    var = jnp.mean(vc * vc, axis=-1, keepdims=True)
    vn = (vc * lax.rsqrt(var + EPS) * g_ref[...] + b_ref[...]).astype(BF16)
    gw = vn.shape[1] // groups
    ri = lax.broadcasted_iota(jnp.int32, (CHUNK, CHUNK), 0)
    ci = lax.broadcasted_iota(jnp.int32, (CHUNK, CHUNK), 1)
    tril = ci <= ri
    bspt = bspt_ref[...]
    for g in range(groups):
        wm = jnp.where(tril, wsp_ref[g], 0.0).astype(BF16)
        bias = bspt[:, g:g + 1]
        for c in range(rows // CHUNK):
            rs = slice(c * CHUNK, (c + 1) * CHUNK)
            cs = slice(g * gw, (g + 1) * gw)
            s = jnp.dot(wm, vn[rs, cs], preferred_element_type=F32) + bias
            o_ref[rs, cs] = (u_ref[rs, cs] * s).astype(o_ref.dtype)


def spatial_gate(uv, ln_g, ln_b, w_sp, b_sp, *, rows=512):
    n = uv.shape[0]
    w = uv.shape[1] // 2
    groups = w_sp.shape[0]
    rows = _row_tile(n, rows)
    return pl.pallas_call(
        functools.partial(_spatial_gate_kernel, groups=groups, rows=rows),
        grid=(n // rows,),
        in_specs=[pl.BlockSpec((rows, w), lambda i: (i, 0)),
                  pl.BlockSpec((rows, w), lambda i: (i, 1)),
                  pl.BlockSpec((1, w), lambda i: (0, 0)),
                  pl.BlockSpec((1, w), lambda i: (0, 0)),
                  pl.BlockSpec((groups, CHUNK, CHUNK), lambda i: (0, 0, 0)),
                  pl.BlockSpec((CHUNK, groups), lambda i: (0, 0))],
        out_specs=pl.BlockSpec((rows, w), lambda i: (i, 0)),
        out_shape=jax.ShapeDtypeStruct((n, w), BF16),
        compiler_params=_params("parallel"),
        name="spatial_gate",
    )(uv, uv, ln_g.reshape(1, w), ln_b.reshape(1, w), w_sp, b_sp.T)


HALO = 32


SUBLANES = 8


def _conv_kernel(cur_ref, halo_ref, w_ref, b_ref, o_ref, xs_ref, *, tt, rb):
    t = pl.program_id(1)
    halo = halo_ref[0]
    xs_ref[0, 0:HALO, :] = jnp.where(t == 0, jnp.zeros_like(halo), halo)
    xs_ref[0, HALO:HALO + tt, :] = cur_ref[0]
    rows = HALO + tt - SUBLANES
    for rho in range(1, SUBLANES):
        for r0 in range(0, rows, rb):
            nr = min(rb, rows - r0)
            xs_ref[rho, r0:r0 + nr, :] = xs_ref[0, r0 + rho:r0 + rho + nr, :]
    bias = b_ref[...]
    base = HALO - (CONV_WIDTH - 1)
    for r in range(tt // rb):
        acc = jnp.zeros((rb, bias.shape[1]), F32) + bias
        for k in range(CONV_WIDTH):
            rho, m = (base + k) % SUBLANES, (base + k) // SUBLANES
            acc = acc + xs_ref[rho, pl.ds(r * rb + SUBLANES * m, rb), :] * w_ref[pl.ds(k, 1), :]
        o_ref[0, r * rb:(r + 1) * rb, :] = acc


def causal_conv(x, w, b, *, tt=512, cb=256, rb=64):
    bsz, t, c = x.shape
    tt = _row_tile(t, tt)
    return pl.pallas_call(
        functools.partial(_conv_kernel, tt=tt, rb=rb),
        grid=(bsz, t // tt, c // cb),
        in_specs=[pl.BlockSpec((1, tt, cb), lambda bi, ti, ci: (bi, ti, ci)),
                  pl.BlockSpec((1, HALO, cb),
                               lambda bi, ti, ci: (bi, jnp.maximum(ti * (tt // HALO) - 1, 0), ci)),
                  pl.BlockSpec((CONV_WIDTH, cb), lambda bi, ti, ci: (0, ci)),
                  pl.BlockSpec((1, cb), lambda bi, ti, ci: (0, ci))],
        out_specs=pl.BlockSpec((1, tt, cb), lambda bi, ti, ci: (bi, ti, ci)),
        out_shape=jax.ShapeDtypeStruct((bsz, t, c), F32),
        scratch_shapes=[pltpu.VMEM((SUBLANES, HALO + tt, cb), F32)],
        compiler_params=_params("parallel", "parallel", "parallel"),
        name="causal_conv",
    )(x, x, w, b.reshape(1, c))


PHALO = 16


def _pool_kernel(cur_ref, halo_ref, w_ref, sc_ref, o_ref, xpad_ref, *, tt, gw):
    t = pl.program_id(1)
    halo = halo_ref[0]
    xpad_ref[0:PHALO, :] = jnp.where(t == 0, jnp.zeros_like(halo), halo)
    xpad_ref[PHALO:PHALO + tt, :] = cur_ref[0]
    pos = t * tt + lax.broadcasted_iota(jnp.int32, (tt, 1), 0)
    for g, win in enumerate(POOL_WINDOWS):
        cs = slice(g * gw, (g + 1) * gw)
        p = xpad_ref[PHALO:PHALO + tt, cs]
        s = p
        for j in range(1, win):
            s = s + xpad_ref[PHALO - j:PHALO - j + tt, cs]
        cnt = jnp.minimum(pos + 1, win).astype(F32)
        d = (s / cnt - p).astype(BF16)
        y = jnp.dot(d, w_ref[g], preferred_element_type=F32) * sc_ref[:, cs]
        o_ref[0, :, cs] = y.astype(o_ref.dtype)


def multiscale_pool(z, w_pool, scale, *, width, tt=256):
    bsz, t, _ = z.shape
    groups = len(POOL_WINDOWS)
    gw = width // groups
    tt = _row_tile(t, tt)
    return pl.pallas_call(
        functools.partial(_pool_kernel, tt=tt, gw=gw),
        grid=(bsz, t // tt),
        in_specs=[pl.BlockSpec((1, tt, width), lambda bi, ti: (bi, ti, 0)),
                  pl.BlockSpec((1, PHALO, width),
                               lambda bi, ti: (bi, jnp.maximum(ti * (tt // PHALO) - 1, 0), 0)),
                  pl.BlockSpec((groups, gw, gw), lambda bi, ti: (0, 0, 0)),
                  pl.BlockSpec((1, width), lambda bi, ti: (0, 0))],
        out_specs=pl.BlockSpec((1, tt, width), lambda bi, ti: (bi, ti, 0)),
        out_shape=jax.ShapeDtypeStruct((bsz, t, width), BF16),
        scratch_shapes=[pltpu.VMEM((PHALO + tt, width), F32)],
        compiler_params=_params("parallel", "parallel"),
        name="multiscale_pool",
    )(z, z, w_pool, scale.reshape(1, width))


def _select_kernel(q_ref, w_ref, ke_ref, ko_ref, o_ref, key_ref, *, tq, ts, nkb, npair, topk, wscale):
    qi = pl.program_id(1)
    t0 = qi * tq
    nsb = (t0 + tq + ts - 1) // ts
    w = w_ref[0] * wscale
    trow = t0 + lax.broadcasted_iota(jnp.int32, (tq, ts), 0)
    scol = lax.broadcasted_iota(jnp.int32, (tq, ts), 1)

    def score_body(kb, c):
        s0 = pl.multiple_of(kb * ts, ts)
        ke = ke_ref[0, pl.ds(s0, ts), :]
        ko = ko_ref[0, pl.ds(s0, ts), :]
        acc = jnp.zeros((tq, ts), F32)
        for j in range(npair):
            slab = q_ref[0, :, j * 128:(j + 1) * 128]
            le = lax.dot_general(slab, ke, _NT, preferred_element_type=F32)
            lo = lax.dot_general(slab, ko, _NT, preferred_element_type=F32)
            acc = acc + jnp.maximum(le, 0.0) * w[:, 2 * j:2 * j + 1]
            acc = acc + jnp.maximum(lo, 0.0) * w[:, 2 * j + 1:2 * j + 2]
        acc = jnp.where(s0 + scol <= trow, acc, -jnp.inf)
        bits = pltpu.bitcast(acc, jnp.int32)
        key_ref[kb] = jnp.where(bits < 0, bits ^ jnp.int32(0x7FFFFFFF), bits)
        return c

    lax.fori_loop(0, nsb, score_body, 0)

    def bit_body(it, tu):
        bit = jnp.left_shift(jnp.int32(1), 31 - it)
        cand_u = tu | bit
        cand = cand_u ^ jnp.int32(INT_MIN)

        def count_body(kb, cnt):
            for l in range(ts // 128):
                cnt = cnt + jnp.where(key_ref[kb, :, l * 128:(l + 1) * 128] >= cand, 1.0, 0.0)
            return cnt

        cnt = lax.fori_loop(0, nsb, count_body, jnp.zeros((tq, 128), F32))
        total = jnp.sum(cnt, axis=1, keepdims=True)
        return jnp.where(total >= float(topk), cand_u, tu)

    tu = lax.fori_loop(0, 32, bit_body, jnp.zeros((tq, 128), jnp.int32))
    thr = jnp.concatenate([tu ^ jnp.int32(INT_MIN)] * (ts // 128), axis=1)

    for kb in range(nkb):
        cols = slice(kb * ts, (kb + 1) * ts)

        @pl.when(kb < nsb)
        def _():
            sel = (key_ref[kb] >= thr) & (kb * ts + scol <= trow)
            o_ref[0, :, cols] = jnp.where(sel, 0.0, NEG).astype(o_ref.dtype)

        @pl.when(kb >= nsb)
        def _():
            o_ref[0, :, cols] = jnp.full((tq, ts), NEG, o_ref.dtype)


def select_mask(q_idx, w_idx, k_even, k_odd, *, n_heads, topk, tq=QUERY_TILE, ts=FAR_BLOCK):
    bsz, t, hd = q_idx.shape
    ts = _row_tile(t, ts)
    nkb = t // ts
    npair = hd // 128
    wscale = float(n_heads ** -0.5 * (hd // n_heads) ** -0.5)
    return pl.pallas_call(
        functools.partial(_select_kernel, tq=tq, ts=ts, nkb=nkb, npair=npair, topk=topk, wscale=wscale),
        grid=(bsz, t // tq),
        in_specs=[pl.BlockSpec((1, tq, hd), lambda b, i: (b, i, 0)),
                  pl.BlockSpec((1, tq, 128), lambda b, i: (b, i, 0)),
                  pl.BlockSpec((1, t, 128), lambda b, i: (b, 0, 0)),
                  pl.BlockSpec((1, t, 128), lambda b, i: (b, 0, 0))],
        out_specs=pl.BlockSpec((1, tq, t), lambda b, i: (b, i, 0)),
        out_shape=jax.ShapeDtypeStruct((bsz, t, t), BF16),
        scratch_shapes=[pltpu.VMEM((nkb, tq, ts), jnp.int32)],
        compiler_params=_params("parallel", "arbitrary"),
        name="select_mask",
    )(q_idx, w_idx, k_even, k_odd)


def _t5_bucket_table(n_max):
    n = np.arange(n_max)
    max_exact = REL_BUCKETS // 2
    nf = np.maximum(n, 1).astype(np.float32)
    ratio = np.log(nf / np.float32(max_exact)) / np.float32(math.log(REL_MAX_DIST / max_exact))
    large = max_exact + (ratio * np.float32(REL_BUCKETS - max_exact)).astype(np.int32)
    large = np.minimum(large, REL_BUCKETS - 1)
    return np.where(n < max_exact, n, large).astype(np.int32)


def _bias_kernel(rb_ref, o_ref, *, tq, thresholds):
    i = lax.broadcasted_iota(jnp.int32, (tq, 2 * tq), 0)
    j = lax.broadcasted_iota(jnp.int32, (tq, 2 * tq), 1)
    nh = o_ref.shape[1]
    for v in range(2):
        d = jnp.maximum(v * tq + i - j, 0)
        bucket = jnp.zeros_like(d)
        for th in thresholds:
            bucket = bucket + jnp.where(d >= th, 1, 0)
        for h in range(nh):
            val = jnp.zeros((tq, 2 * tq), F32)
            for b in range(REL_BUCKETS):
                val = jnp.where(bucket == b, rb_ref[b, h] - rb_ref[REL_BUCKETS - 1, h], val)
            o_ref[v, h] = val


def bias_windows(rel_bias, *, tq=QUERY_TILE):
    table = _t5_bucket_table(4 * tq)
    assert np.all(np.diff(table) >= 0) and np.all(np.diff(table) <= 1)
    assert np.all(table[tq:] == REL_BUCKETS - 1), "bias must be constant beyond the near window"
    thresholds = tuple(int(np.argmax(table >= b)) for b in range(1, REL_BUCKETS))
    nh = rel_bias.shape[1]
    return pl.pallas_call(
        functools.partial(_bias_kernel, tq=tq, thresholds=thresholds),
        in_specs=[pl.BlockSpec(memory_space=pltpu.SMEM)],
        out_specs=pl.BlockSpec(memory_space=pltpu.VMEM),
        out_shape=jax.ShapeDtypeStruct((2, nh, tq, 2 * tq), F32),
        compiler_params=pltpu.CompilerParams(vmem_limit_bytes=VMEM_LIMIT),
        name="bias_windows",
    )(rel_bias)


def _attn_kernel(q_ref, madd_ref, ckv_ref, wuk_ref, wuv_ref, btab_ref, o_ref,
                 qlat_ref, m_ref, l_ref, acc_ref, s0_ref, s1_ref, p0_ref, p1_ref, a0_ref, a1_ref,
                 *, tq, nh, dh, fb, scale):
    qi = pl.program_id(1)
    t0 = qi * tq
    for h in range(nh):
        qh = q_ref[0, :, h * dh:(h + 1) * dh]
        qlat_ref[h * tq:(h + 1) * tq, :] = jnp.dot(
            qh, wuk_ref[h], preferred_element_type=F32).astype(BF16)
    m_ref[...] = jnp.full(m_ref.shape, NEG, F32)
    l_ref[...] = jnp.zeros(l_ref.shape, F32)
    acc_ref[...] = jnp.zeros(acc_ref.shape, F32)
    rep = acc_ref.shape[1] // 128
    nw = 2 * tq
    s_ref, p_ref, alpha_ref = (s0_ref, s1_ref), (p0_ref, p1_ref), (a0_ref, a1_ref)

    def qk(kv, slot):
        s_ref[slot][:, :kv.shape[0]] = lax.dot_general(
            qlat_ref[...], kv, _NT, preferred_element_type=F32)

    def softmax(slot, n, ma, bias_of_head):
        for h in range(nh):
            rows = slice(h * tq, (h + 1) * tq)
            lg = s_ref[slot][rows, :n] * scale + ma
            if bias_of_head is not None:
                lg = lg + bias_of_head(h)
            m_old = m_ref[rows, :1]
            m_new = jnp.maximum(m_old, jnp.max(lg, axis=1, keepdims=True))
            alpha = jnp.exp(m_old - m_new)
            p = jnp.exp(lg - m_new)
            l_new = alpha * l_ref[rows, :1] + jnp.sum(p, axis=1, keepdims=True)
            p_ref[slot][rows, :n] = p.astype(BF16)
            alpha_ref[slot][rows, :] = jnp.broadcast_to(alpha, (tq, 128))
            m_ref[rows, :] = jnp.broadcast_to(m_new, (tq, 128))
            l_ref[rows, :] = jnp.broadcast_to(l_new, (tq, 128))

    def pv(kv, slot):
        n = kv.shape[0]
        upd = jnp.dot(p_ref[slot][:, :n], kv, preferred_element_type=F32)
        acc_ref[...] = acc_ref[...] * jnp.concatenate([alpha_ref[slot][...]] * rep, axis=1) + upd

    lim = t0 - tq
    nfar = (jnp.maximum(lim, 0) + fb - 1) // fb
    scol = lax.broadcasted_iota(jnp.int32, (tq, fb), 1)

    def far_kv(kb):
        return ckv_ref[0, pl.ds(pl.multiple_of(kb * fb, fb), fb), :]

    w0 = pl.multiple_of(jnp.maximum(lim, 0), tq)
    variant = jnp.minimum(qi, 1)
    kv_near = ckv_ref[0, pl.ds(w0, nw), :]
    qk(kv_near, 1)
    softmax(1, nw, madd_ref[0, :, pl.ds(w0, nw)].astype(F32), lambda h: btab_ref[variant, h])
    pv(kv_near, 1)
    qk(far_kv(0), 0)
    p_ref[1][...] = jnp.zeros(p_ref[1].shape, BF16)
    alpha_ref[1][...] = jnp.ones(alpha_ref[1].shape, F32)

    def far_stages(kb, slot):
        s0 = pl.multiple_of(kb * fb, fb)
        pv(far_kv(jnp.maximum(kb - 1, 0)), 1 - slot)
        qk(far_kv(jnp.minimum(kb + 1, nfar - 1)), 1 - slot)
        ma = madd_ref[0, :, pl.ds(s0, fb)].astype(F32)
        softmax(slot, fb, jnp.where(s0 + scol < lim, ma, NEG), None)

    def far_body(kb, c):
        for slot in range(2):
            pl.when(kb % 2 == slot)(functools.partial(far_stages, kb, slot))
        return c

    lax.fori_loop(0, nfar, far_body, 0)
    last = jnp.maximum(nfar - 1, 0)
    last_slot = jnp.where(nfar > 0, last % 2, 1)
    for slot in range(2):
        pl.when(last_slot == slot)(functools.partial(pv, far_kv(last), slot))

    for h in range(nh):
        rows = slice(h * tq, (h + 1) * tq)
        o = acc_ref[rows, :] / l_ref[rows, :1]
        y = jnp.dot(o.astype(BF16), wuv_ref[h], preferred_element_type=F32)
        o_ref[0, :, h * dh:(h + 1) * dh] = y.astype(o_ref.dtype)


def masked_latent_attention(q, madd, c_kv, w_uk, w_uv, btab, *, tq=QUERY_TILE, fb=FAR_BLOCK):
    bsz, t, hd = q.shape
    nh, dh, r = w_uk.shape
    fb = _row_tile(t, fb)
    assert t >= fb >= 2 * tq and t % tq == 0
    once = pl.Buffered(1)
    return pl.pallas_call(
        functools.partial(_attn_kernel, tq=tq, nh=nh, dh=dh, fb=fb, scale=float(dh ** -0.5)),
        grid=(bsz, t // tq),
        in_specs=[pl.BlockSpec((1, tq, hd), lambda b, i: (b, i, 0)),
                  pl.BlockSpec((1, tq, t), lambda b, i: (b, i, 0)),
                  pl.BlockSpec((1, t, r), lambda b, i: (b, 0, 0), pipeline_mode=once),
                  pl.BlockSpec((nh, dh, r), lambda b, i: (0, 0, 0), pipeline_mode=once),
                  pl.BlockSpec((nh, r, dh), lambda b, i: (0, 0, 0), pipeline_mode=once),
                  pl.BlockSpec((2, nh, tq, 2 * tq), lambda b, i: (0, 0, 0, 0), pipeline_mode=once)],
        out_specs=pl.BlockSpec((1, tq, hd), lambda b, i: (b, i, 0)),
        out_shape=jax.ShapeDtypeStruct((bsz, t, hd), BF16),
        scratch_shapes=[pltpu.VMEM((nh * tq, r), BF16),
                        pltpu.VMEM((nh * tq, 128), F32),
                        pltpu.VMEM((nh * tq, 128), F32),
                        pltpu.VMEM((nh * tq, r), F32),
                        pltpu.VMEM((nh * tq, fb), F32),
                        pltpu.VMEM((nh * tq, fb), F32),
                        pltpu.VMEM((nh * tq, fb), BF16),
                        pltpu.VMEM((nh * tq, fb), BF16),
                        pltpu.VMEM((nh * tq, 128), F32),
                        pltpu.VMEM((nh * tq, 128), F32)],
        compiler_params=_params("parallel", "arbitrary"),
        name="masked_latent_attention",
    )(q, madd, c_kv, w_uk, w_uv, btab)


def _ffn(x, h, w_in, w_out, lead, next_gain):
    dff = w_out.shape[-2]
    a = dual_matmul(h, WSlab(w_in, lead, cols=dff), WSlab(w_in, lead, col0=dff, cols=dff), BF16,
                    act="swiglu", tm=2048, tn=256, single_buffer_a=True)
    return matmul_residual([(a, WSlab(w_out, lead))], x, 0.5, tm=512, tn=512, next_gain=next_gain)


def _mixer_ab(x, h, bsz, t, i, w_in, ln_a_g, ln_a_b, w_sp, b_sp, conv_w, conv_b, ln_b_g, ln_b_b, w_out,
              next_gain):
    wa = ln_a_g.shape[0]
    wb = ln_b_g.shape[0]
    uv = matmul(h, WSlab(w_in, (i,), cols=2 * wa), F32, act="gelu", tm=1024, tn=512)
    y_a = spatial_gate(uv, ln_a_g, ln_a_b, w_sp, b_sp)
    glu = dual_matmul(h, WSlab(w_in, (i,), col0=2 * wa, cols=wb),
                      WSlab(w_in, (i,), col0=2 * wa + wb, cols=wb), F32, act="glu", tm=1024, tn=256)
    conv = causal_conv(glu.reshape(bsz, t, wb), conv_w, conv_b).reshape(bsz * t, wb)
    y_b = ln_silu(conv, ln_b_g, ln_b_b, BF16)
    return matmul_residual([(y_a, WSlab(w_out, (i,), rows=wa)),
                            (y_b, WSlab(w_out, (i,), row0=wa, rows=wb))], x, 1.0, tm=1024, tn=512,
                           next_gain=next_gain)


def _mixer_cd(x, h, bsz, t, i, w_in, w_pool, pool_scale, g_cq, w_uq, w_qidx, g_ckv, w_uk, w_uv,
              rel_bias, w_out, next_gain):
    n = bsz * t
    wc = pool_scale.shape[0]
    qr = g_cq.shape[0]
    kvr = g_ckv.shape[0]
    n_heads, dh, _ = w_uk.shape
    idx_dim = 64
    idx_heads = w_qidx.shape[-1] // idx_dim
    o3 = wc + qr + kvr
    assert wc % qr == 0 and (wc + qr) % kvr == 0 and idx_heads <= 128 and 2 * idx_dim == 128

    z = matmul(h, WSlab(w_in, (i,), cols=o3), F32, tm=1024, tn=512)
    wk = w_in[i, :, o3:o3 + idx_dim]
    ww = w_in[i, :, o3 + idx_dim:]
    zk = jnp.zeros_like(wk)
    w_small = jnp.concatenate(
        [wk, zk, zk, wk, ww, jnp.zeros((wk.shape[0], 128 - idx_heads), wk.dtype)], axis=1)
    zs = matmul(h, WSlab(w_small), F32, tm=1024, tn=384)
    k_even = zs[:, :128].astype(BF16).reshape(bsz, t, 128)
    k_odd = zs[:, 128:256].astype(BF16).reshape(bsz, t, 128)
    w_idx = zs[:, 256:].reshape(bsz, t, 128)

    y_c = multiscale_pool(z.reshape(bsz, t, o3), w_pool.astype(BF16), pool_scale, width=wc)
    c_q = Act(rms_norm(z, g_cq, BF16, col_block=wc // qr, width=qr))
    c_kv = rms_norm(z, g_ckv, BF16, col_block=(wc + qr) // kvr, width=kvr)
    q = matmul(c_q, WSlab(w_uq, (i,)), BF16, tm=1024, tn=512)
    q_idx = matmul(c_q, WSlab(w_qidx, (i,)), BF16, tm=1024, tn=512)

    topk = min(TOPK_MAX, t // 4)
    madd = select_mask(q_idx.reshape(bsz, t, -1), w_idx, k_even, k_odd, n_heads=idx_heads, topk=topk)
    btab = bias_windows(rel_bias)
    y_d = masked_latent_attention(q.reshape(bsz, t, -1), madd, c_kv.reshape(bsz, t, kvr),
                                  w_uk.astype(BF16), w_uv.astype(BF16), btab)
    return matmul_residual([(y_c.reshape(n, wc), WSlab(w_out, (i,), rows=wc)),
                            (y_d.reshape(n, -1), WSlab(w_out, (i,), row0=wc, rows=n_heads * dh))],
                           x, 1.0, tm=1024, tn=512, next_gain=next_gain)


def kernel(x, g_ff, w_ff_in, w_ff_out, g_mix, w_in_ab, ln_a_g, ln_a_b, w_sp, b_sp, conv_w, conv_b,
           ln_b_g, ln_b_b, w_out_ab, w_in_cd, w_pool, pool_scale, g_cq, w_uq, w_qidx, g_ckv, w_uk,
           w_uv, rel_bias, w_out_cd, g_final):
    bsz, t, d = x.shape
    depth = g_ff.shape[0]
    xs = x.reshape(bsz * t, d)
    w_ff_out = w_ff_out.astype(BF16)
    h = Act(rms_norm(xs, g_ff[0, 0], BF16))
    for l in range(depth):
        xs, h = _ffn(xs, h, w_ff_in, w_ff_out, (l, 0), g_mix[l])
        i = l // 2
        if l % 2 == 0:
            xs, h = _mixer_ab(xs, h, bsz, t, i, w_in_ab, ln_a_g[i], ln_a_b[i], w_sp[i], b_sp[i],
                              conv_w[i], conv_b[i], ln_b_g[i], ln_b_b[i], w_out_ab, g_ff[l, 1])
        else:
            xs, h = _mixer_cd(xs, h, bsz, t, i, w_in_cd, w_pool[i], pool_scale[i], g_cq[i], w_uq,
                              w_qidx, g_ckv[i], w_uk[i], w_uv[i], rel_bias, w_out_cd, g_ff[l, 1])
        xs, h = _ffn(xs, h, w_ff_in, w_ff_out, (l, 1), g_ff[l + 1, 0] if l + 1 < depth else None)
    return rms_norm(xs, g_final, x.dtype).reshape(bsz, t, d)
```

```python
import functools
import math
from typing import NamedTuple, Optional

import numpy as np
import jax
import jax.numpy as jnp
from jax import lax
from jax.experimental import pallas as pl
from jax.experimental.pallas import tpu as pltpu

F32 = jnp.float32
BF16 = jnp.bfloat16

EPS = 1e-6
CHUNK = 128
CONV_WIDTH = 31
POOL_WINDOWS = (2, 4, 8, 16)
TOPK_MAX = 256
REL_BUCKETS = 32
REL_MAX_DIST = 128
QUERY_TILE = 128
FAR_BLOCK = 512
NEG = -1e30
INT_MIN = -2147483648

V7X_VMEM_BYTES = 64 * 1024 * 1024
VMEM_LIMIT = V7X_VMEM_BYTES - 8 * 1024 * 1024

_NT = (((1,), (1,)), ((), ()))


def _params(*sem):
    return pltpu.CompilerParams(dimension_semantics=sem, vmem_limit_bytes=VMEM_LIMIT)


def _row_tile(n, want):
    t = min(n, want)
    assert n % t == 0, (n, t)
    return t


def _rms_kernel(x_ref, g_ref, o_ref):
    x = x_ref[...].astype(F32)
    y = x * lax.rsqrt(jnp.mean(x * x, axis=-1, keepdims=True) + EPS)
    o_ref[...] = (y * g_ref[...]).astype(o_ref.dtype)


def rms_norm(x, g, out_dtype, *, col_block=0, width=None, tm=256):
    n = x.shape[0]
    width = x.shape[1] if width is None else width
    tm = _row_tile(n, tm)
    return pl.pallas_call(
        _rms_kernel,
        grid=(n // tm,),
        in_specs=[pl.BlockSpec((tm, width), lambda i: (i, col_block)),
                  pl.BlockSpec((1, width), lambda i: (0, 0))],
        out_specs=pl.BlockSpec((tm, width), lambda i: (i, 0)),
        out_shape=jax.ShapeDtypeStruct((n, width), out_dtype),
        compiler_params=_params("parallel"),
        name="rms_norm",
    )(x, g.reshape(1, width).astype(F32))


def _ln_silu_kernel(x_ref, g_ref, b_ref, o_ref):
    x = x_ref[...]
    mu = jnp.mean(x, axis=-1, keepdims=True)
    xc = x - mu
    var = jnp.mean(xc * xc, axis=-1, keepdims=True)
    y = xc * lax.rsqrt(var + EPS) * g_ref[...] + b_ref[...]
    o_ref[...] = (y * jax.nn.sigmoid(y)).astype(o_ref.dtype)


def ln_silu(x, g, b, out_dtype, *, tm=256):
    n, d = x.shape
    tm = _row_tile(n, tm)
    return pl.pallas_call(
        _ln_silu_kernel,
        grid=(n // tm,),
        in_specs=[pl.BlockSpec((tm, d), lambda i: (i, 0)),
                  pl.BlockSpec((1, d), lambda i: (0, 0)),
                  pl.BlockSpec((1, d), lambda i: (0, 0))],
        out_specs=pl.BlockSpec((tm, d), lambda i: (i, 0)),
        out_shape=jax.ShapeDtypeStruct((n, d), out_dtype),
        compiler_params=_params("parallel"),
        name="ln_silu",
    )(x, g.reshape(1, d), b.reshape(1, d))


class WSlab(NamedTuple):
    arr: jax.Array
    lead: tuple = ()
    row0: int = 0
    rows: Optional[int] = None
    col0: int = 0
    cols: Optional[int] = None

    def shape(self):
        r = self.arr.shape[-2] if self.rows is None else self.rows
        c = self.arr.shape[-1] if self.cols is None else self.cols
        return r, c

    def spec(self, tn):
        r, _ = self.shape()
        assert self.row0 % r == 0 and self.col0 % tn == 0, (self.row0, r, self.col0, tn)
        lead, rb, cb = self.lead, self.row0 // r, self.col0 // tn
        return pl.BlockSpec((None,) * len(lead) + (r, tn), lambda i, j: lead + (rb, cb + j))


class Act(NamedTuple):
    a: jax.Array
    ssq: Optional[jax.Array] = None

    def specs(self, tm, single_buffer=False):
        k = self.a.shape[1]
        mode = dict(pipeline_mode=pl.Buffered(1)) if single_buffer else {}
        sp = [pl.BlockSpec((tm, k), lambda i, j: (i, 0), **mode)]
        if self.ssq is not None:
            sp.append(pl.BlockSpec((tm, 128), lambda i, j: (i, 0)))
        return sp

    def args(self):
        return [self.a] if self.ssq is None else [self.a, self.ssq]


MXU_ROWS = 256


def _sub_rows(tm):
    return max(min(tm, MXU_ROWS), tm // 4)


def _row_factor(ssq_ref, rows, k, width):
    rf = lax.rsqrt(ssq_ref[rows, :] * (1.0 / k) + EPS)
    return jnp.concatenate([rf] * (width // 128), axis=1)


def _mm_kernel(*refs, act, normed):
    a_ref, w_ref, o_ref = refs[0], refs[-2], refs[-1]
    w = w_ref[...].astype(BF16)
    tm, k = a_ref.shape
    rb = _sub_rows(tm)
    for r0 in range(0, tm, rb):
        rows = slice(r0, r0 + rb)
        r = jnp.dot(a_ref[rows, :], w, preferred_element_type=F32)
        if normed:
            r = r * _row_factor(refs[1], rows, k, r.shape[1])
        if act == "gelu":
            r = 0.5 * r * (1.0 + lax.erf(r * math.sqrt(0.5)))
        o_ref[rows, :] = r.astype(o_ref.dtype)


def matmul(x, w, out_dtype, *, act=None, tm=1024, tn=512):
    n, k = x.a.shape
    assert w.shape()[0] == k
    nout = w.shape()[1]
    tm = _row_tile(n, tm)
    tn = _row_tile(nout, tn)
    return pl.pallas_call(
        functools.partial(_mm_kernel, act=act, normed=x.ssq is not None),
        grid=(n // tm, nout // tn),
        in_specs=x.specs(tm) + [w.spec(tn)],
        out_specs=pl.BlockSpec((tm, tn), lambda i, j: (i, j)),
        out_shape=jax.ShapeDtypeStruct((n, nout), out_dtype),
        compiler_params=_params("parallel", "arbitrary"),
        name="matmul_" + (act or "plain"),
    )(*x.args(), w.arr)


def _dual_mm_kernel(*refs, act, normed):
    a_ref, w1_ref, w2_ref, o_ref = refs[0], refs[-3], refs[-2], refs[-1]
    w1 = w1_ref[...].astype(BF16)
    w2 = w2_ref[...].astype(BF16)
    tm = a_ref.shape[0]
    rb = _sub_rows(tm)
    for r0 in range(0, tm, rb):
        rows = slice(r0, r0 + rb)
        a = a_ref[rows, :]
        p = jnp.dot(a, w1, preferred_element_type=F32)
        q = jnp.dot(a, w2, preferred_element_type=F32)
        if normed:
            rf = _row_factor(refs[1], rows, a_ref.shape[1], p.shape[1])
            p, q = p * rf, q * rf
        if act == "swiglu":
            r = p * jax.nn.sigmoid(p) * q
        else:
            r = p * jax.nn.sigmoid(q)
        o_ref[rows, :] = r.astype(o_ref.dtype)


def dual_matmul(x, w1, w2, out_dtype, *, act, tm=1024, tn=256, single_buffer_a=False):
    n, k = x.a.shape
    assert w1.shape() == w2.shape() and w1.shape()[0] == k
    nout = w1.shape()[1]
    tm = _row_tile(n, tm)
    tn = _row_tile(nout, tn)
    return pl.pallas_call(
        functools.partial(_dual_mm_kernel, act=act, normed=x.ssq is not None),
        grid=(n // tm, nout // tn),
        in_specs=x.specs(tm, single_buffer_a) + [w1.spec(tn), w2.spec(tn)],
        out_specs=pl.BlockSpec((tm, tn), lambda i, j: (i, j)),
        out_shape=jax.ShapeDtypeStruct((n, nout), out_dtype),
        compiler_params=_params("parallel", "arbitrary"),
        name="dual_matmul_" + act,
    )(*x.args(), w1.arr, w2.arr)


def _mm_res_kernel(*refs, npairs, scale, emit_norm):
    res_ref = refs[2 * npairs]
    if emit_norm:
        g_ref, o_ref, xg_ref, ssq_ref = refs[2 * npairs + 1:]
    else:
        o_ref = refs[2 * npairs + 1]
    tm, tn = o_ref.shape
    ws = [refs[2 * p + 1][...].astype(BF16) for p in range(npairs)]
    rb = _sub_rows(tm)
    parts = []
    for r0 in range(0, tm, rb):
        rows = slice(r0, r0 + rb)
        acc = jnp.dot(refs[0][rows, :], ws[0], preferred_element_type=F32)
        for p in range(1, npairs):
            acc = acc + jnp.dot(refs[2 * p][rows, :], ws[p], preferred_element_type=F32)
        y = res_ref[rows, :] + scale * acc
        o_ref[rows, :] = y
        if emit_norm:
            xg_ref[rows, :] = (y * g_ref[...]).astype(xg_ref.dtype)
            parts.append(jnp.broadcast_to(jnp.sum(y * y, axis=1, keepdims=True), (rb, ssq_ref.shape[1])))
    if not emit_norm:
        return
    part = jnp.concatenate(parts, axis=0)
    j = pl.program_id(1)

    @pl.when(j == 0)
    def _():
        ssq_ref[...] = part

    @pl.when(j > 0)
    def _():
        ssq_ref[...] += part


def matmul_residual(pairs, res, scale, *, tm, tn, next_gain=None):
    n, nout = res.shape
    tm = _row_tile(n, tm)
    tn = _row_tile(nout, tn)
    in_specs, args = [], []
    for a, w in pairs:
        k = a.shape[1]
        assert w.shape() == (k, nout)
        in_specs += [pl.BlockSpec((tm, k), lambda i, j: (i, 0)), w.spec(tn)]
        args += [a, w.arr]
    in_specs.append(pl.BlockSpec((tm, tn), lambda i, j: (i, j)))
    args.append(res)
    out_specs = pl.BlockSpec((tm, tn), lambda i, j: (i, j))
    out_shape = jax.ShapeDtypeStruct((n, nout), F32)
    if next_gain is not None:
        in_specs.append(pl.BlockSpec((1, tn), lambda i, j: (0, j)))
        args.append(next_gain.reshape(1, nout).astype(F32))
        out_specs = [out_specs, pl.BlockSpec((tm, tn), lambda i, j: (i, j)),
                     pl.BlockSpec((tm, 128), lambda i, j: (i, 0))]
        out_shape = [out_shape, jax.ShapeDtypeStruct((n, nout), BF16),
                     jax.ShapeDtypeStruct((n, 128), F32)]
    out = pl.pallas_call(
        functools.partial(_mm_res_kernel, npairs=len(pairs), scale=scale,
                          emit_norm=next_gain is not None),
        grid=(n // tm, nout // tn),
        in_specs=in_specs,
        out_specs=out_specs,
        out_shape=out_shape,
        compiler_params=_params("parallel", "arbitrary"),
        name="matmul_residual",
    )(*args)
    if next_gain is None:
        return out, None
    return out[0], Act(out[1], out[2])


def _spatial_gate_kernel(u_ref, v_ref, g_ref, b_ref, wsp_ref, bspt_ref, o_ref, *, groups, rows):
    v = v_ref[...]
    mu = jnp.mean(v, axis=-1, keepdims=True)
    vc = v - mu
    var = jnp.mean(vc * vc, axis=-1, keepdims=True)
    vn = (vc * lax.rsqrt(var + EPS) * g_ref[...] + b_ref[...]).astype(BF16)
    gw = vn.shape[1] // groups
    ri = lax.broadcasted_iota(jnp.int32, (CHUNK, CHUNK), 0)
    ci = lax.broadcasted_iota(jnp.int32, (CHUNK, CHUNK), 1)
    tril = ci <= ri
    bspt = bspt_ref[...]
    for g in range(groups):
        wm = jnp.where(tril, wsp_ref[g], 0.0).astype(BF16)
        bias = bspt[:, g:g + 1]
        for c in range(rows // CHUNK):
            rs = slice(c * CHUNK, (c + 1) * CHUNK)
            cs = slice(g * gw, (g + 1) * gw)
            s = jnp.dot(wm, vn[rs, cs], preferred_element_type=F32) + bias
            o_ref[rs, cs] = (u_ref[rs, cs] * s).astype(o_ref.dtype)


def spatial_gate(uv, ln_g, ln_b, w_sp, b_sp, *, rows=512):
    n = uv.shape[0]
    w = uv.shape[1] // 2
    groups = w_sp.shape[0]
    rows = _row_tile(n, rows)
    return pl.pallas_call(
        functools.partial(_spatial_gate_kernel, groups=groups, rows=rows),
        grid=(n // rows,),
        in_specs=[pl.BlockSpec((rows, w), lambda i: (i, 0)),
                  pl.BlockSpec((rows, w), lambda i: (i, 1)),
                  pl.BlockSpec((1, w), lambda i: (0, 0)),
                  pl.BlockSpec((1, w), lambda i: (0, 0)),
                  pl.BlockSpec((groups, CHUNK, CHUNK), lambda i: (0, 0, 0)),
                  pl.BlockSpec((CHUNK, groups), lambda i: (0, 0))],
        out_specs=pl.BlockSpec((rows, w), lambda i: (i, 0)),
        out_shape=jax.ShapeDtypeStruct((n, w), BF16),
        compiler_params=_params("parallel"),
        name="spatial_gate",
    )(uv, uv, ln_g.reshape(1, w), ln_b.reshape(1, w), w_sp, b_sp.T)


HALO = 32


SUBLANES = 8


def _conv_kernel(cur_ref, halo_ref, w_ref, b_ref, o_ref, xs_ref, *, tt, rb):
    t = pl.program_id(1)
    halo = halo_ref[0]
    xs_ref[0, 0:HALO, :] = jnp.where(t == 0, jnp.zeros_like(halo), halo)
    xs_ref[0, HALO:HALO + tt, :] = cur_ref[0]
    rows = HALO + tt - SUBLANES
    for rho in range(1, SUBLANES):
        for r0 in range(0, rows, rb):
            nr = min(rb, rows - r0)
            xs_ref[rho, r0:r0 + nr, :] = xs_ref[0, r0 + rho:r0 + rho + nr, :]
    bias = b_ref[...]
    base = HALO - (CONV_WIDTH - 1)
    for r in range(tt // rb):
        acc = jnp.zeros((rb, bias.shape[1]), F32) + bias
        for k in range(CONV_WIDTH):
            rho, m = (base + k) % SUBLANES, (base + k) // SUBLANES
            acc = acc + xs_ref[rho, pl.ds(r * rb + SUBLANES * m, rb), :] * w_ref[pl.ds(k, 1), :]
        o_ref[0, r * rb:(r + 1) * rb, :] = acc


def causal_conv(x, w, b, *, tt=512, cb=256, rb=64):
    bsz, t, c = x.shape
    tt = _row_tile(t, tt)
    return pl.pallas_call(
        functools.partial(_conv_kernel, tt=tt, rb=rb),
        grid=(bsz, t // tt, c // cb),
        in_specs=[pl.BlockSpec((1, tt, cb), lambda bi, ti, ci: (bi, ti, ci)),
                  pl.BlockSpec((1, HALO, cb),
                               lambda bi, ti, ci: (bi, jnp.maximum(ti * (tt // HALO) - 1, 0), ci)),
                  pl.BlockSpec((CONV_WIDTH, cb), lambda bi, ti, ci: (0, ci)),
                  pl.BlockSpec((1, cb), lambda bi, ti, ci: (0, ci))],
        out_specs=pl.BlockSpec((1, tt, cb), lambda bi, ti, ci: (bi, ti, ci)),
        out_shape=jax.ShapeDtypeStruct((bsz, t, c), F32),
        scratch_shapes=[pltpu.VMEM((SUBLANES, HALO + tt, cb), F32)],
        compiler_params=_params("parallel", "parallel", "parallel"),
        name="causal_conv",
    )(x, x, w, b.reshape(1, c))


PHALO = 16


def _pool_kernel(cur_ref, halo_ref, w_ref, sc_ref, o_ref, xpad_ref, *, tt, gw):
    t = pl.program_id(1)
    halo = halo_ref[0]
    xpad_ref[0:PHALO, :] = jnp.where(t == 0, jnp.zeros_like(halo), halo)
    xpad_ref[PHALO:PHALO + tt, :] = cur_ref[0]
    pos = t * tt + lax.broadcasted_iota(jnp.int32, (tt, 1), 0)
    for g, win in enumerate(POOL_WINDOWS):
        cs = slice(g * gw, (g + 1) * gw)
        p = xpad_ref[PHALO:PHALO + tt, cs]
        s = p
        for j in range(1, win):
            s = s + xpad_ref[PHALO - j:PHALO - j + tt, cs]
        cnt = jnp.minimum(pos + 1, win).astype(F32)
        d = (s / cnt - p).astype(BF16)
        y = jnp.dot(d, w_ref[g], preferred_element_type=F32) * sc_ref[:, cs]
        o_ref[0, :, cs] = y.astype(o_ref.dtype)


def multiscale_pool(z, w_pool, scale, *, width, tt=256):
    bsz, t, _ = z.shape
    groups = len(POOL_WINDOWS)
    gw = width // groups
    tt = _row_tile(t, tt)
    return pl.pallas_call(
        functools.partial(_pool_kernel, tt=tt, gw=gw),
        grid=(bsz, t // tt),
        in_specs=[pl.BlockSpec((1, tt, width), lambda bi, ti: (bi, ti, 0)),
                  pl.BlockSpec((1, PHALO, width),
                               lambda bi, ti: (bi, jnp.maximum(ti * (tt // PHALO) - 1, 0), 0)),
                  pl.BlockSpec((groups, gw, gw), lambda bi, ti: (0, 0, 0)),
                  pl.BlockSpec((1, width), lambda bi, ti: (0, 0))],
        out_specs=pl.BlockSpec((1, tt, width), lambda bi, ti: (bi, ti, 0)),
        out_shape=jax.ShapeDtypeStruct((bsz, t, width), BF16),
        scratch_shapes=[pltpu.VMEM((PHALO + tt, width), F32)],
        compiler_params=_params("parallel", "parallel"),
        name="multiscale_pool",
    )(z, z, w_pool, scale.reshape(1, width))


def _select_kernel(q_ref, w_ref, ke_ref, ko_ref, o_ref, key_ref, jmax_ref, *, tq, ts, nkb, npair, topk, wscale):
    qi = pl.program_id(1)
    t0 = qi * tq
    nsb = (t0 + tq + ts - 1) // ts
    w = w_ref[0] * wscale
    trow = t0 + lax.broadcasted_iota(jnp.int32, (tq, ts), 0)
    scol = lax.broadcasted_iota(jnp.int32, (tq, ts), 1)

    def score_body(kb, c):
        s0 = pl.multiple_of(kb * ts, ts)
        ke = ke_ref[0, pl.ds(s0, ts), :]
        ko = ko_ref[0, pl.ds(s0, ts), :]
        acc = jnp.zeros((tq, ts), F32)
        for j in range(npair):
            slab = q_ref[0, :, j * 128:(j + 1) * 128]
            le = lax.dot_general(slab, ke, _NT, preferred_element_type=F32)
            lo = lax.dot_general(slab, ko, _NT, preferred_element_type=F32)
            acc = acc + jnp.maximum(le, 0.0) * w[:, 2 * j:2 * j + 1]
            acc = acc + jnp.maximum(lo, 0.0) * w[:, 2 * j + 1:2 * j + 2]
        acc = jnp.where(s0 + scol <= trow, acc + 0.0, -jnp.inf)
        bits = pltpu.bitcast(acc, jnp.int32)
        key_ref[kb] = jnp.where(bits < 0, bits ^ jnp.int32(0x7FFFFFFF), bits)
        return c

    lax.fori_loop(0, nsb, score_body, 0)

    def count_rows(pred_of_block):
        def count_body(kb, cnt):
            for l in range(ts // 128):
                hit = pred_of_block(key_ref[kb, :, l * 128:(l + 1) * 128], kb * ts + l * 128)
                cnt = cnt + jnp.where(hit, 1.0, 0.0)
            return cnt

        cnt = lax.fori_loop(0, nsb, count_body, jnp.zeros((tq, 128), F32))
        return jnp.broadcast_to(jnp.sum(cnt, axis=1, keepdims=True), (tq, 128))

    def bit_body(it, carry):
        tu, n_ge = carry
        bit = jnp.left_shift(jnp.int32(1), 31 - it)
        cand_u = tu | bit
        cand = cand_u ^ jnp.int32(INT_MIN)
        total = count_rows(lambda k, s0: k >= cand)
        keep = total >= float(topk)
        return jnp.where(keep, cand_u, tu), jnp.where(keep, total, n_ge)

    n_all = jnp.full((tq, 128), 1.0, F32) * (nsb * ts).astype(F32)
    tu, n_ge = lax.fori_loop(0, 32, bit_body, (jnp.zeros((tq, 128), jnp.int32), n_all))
    thr128 = tu ^ jnp.int32(INT_MIN)
    thr = jnp.concatenate([thr128] * (ts // 128), axis=1)

    jmax_ref[...] = jnp.full((tq, 128), nkb * ts, jnp.int32)
    lane = lax.broadcasted_iota(jnp.int32, (tq, 128), 1)

    @pl.when(jnp.max(n_ge) > float(topk))
    def _():
        n_eq = count_rows(lambda k, s0: k == thr128)
        need = float(topk) - (n_ge - n_eq)

        nbits = (nkb * ts - 1).bit_length()

        def idx_body(it, x):
            cand = x | jnp.left_shift(jnp.int32(1), nbits - 1 - it)
            below = count_rows(lambda k, s0: (k == thr128) & (s0 + lane < cand))
            return jnp.where(below < need, cand, x)

        jmax_ref[...] = lax.fori_loop(0, nbits, idx_body, jnp.zeros((tq, 128), jnp.int32))

    jmax = jnp.concatenate([jmax_ref[...]] * (ts // 128), axis=1)

    for kb in range(nkb):
        cols = slice(kb * ts, (kb + 1) * ts)

        @pl.when(kb < nsb)
        def _():
            key = key_ref[kb]
            s = kb * ts + scol
            tie = jnp.where(s <= jmax, 0.0, NEG)
            val = jnp.where(key > thr, 0.0, jnp.where(key == thr, tie, NEG))
            o_ref[0, :, cols] = jnp.where(s <= trow, val, NEG).astype(o_ref.dtype)

        @pl.when(kb >= nsb)
        def _():
            o_ref[0, :, cols] = jnp.full((tq, ts), NEG, o_ref.dtype)


def select_mask(q_idx, w_idx, k_even, k_odd, *, n_heads, topk, tq=QUERY_TILE, ts=FAR_BLOCK):
    bsz, t, hd = q_idx.shape
    ts = _row_tile(t, ts)
    nkb = t // ts
    npair = hd // 128
    wscale = float(n_heads ** -0.5 * (hd // n_heads) ** -0.5)
    return pl.pallas_call(
        functools.partial(_select_kernel, tq=tq, ts=ts, nkb=nkb, npair=npair, topk=topk, wscale=wscale),
        grid=(bsz, t // tq),
        in_specs=[pl.BlockSpec((1, tq, hd), lambda b, i: (b, i, 0)),
                  pl.BlockSpec((1, tq, 128), lambda b, i: (b, i, 0)),
                  pl.BlockSpec((1, t, 128), lambda b, i: (b, 0, 0)),
                  pl.BlockSpec((1, t, 128), lambda b, i: (b, 0, 0))],
        out_specs=pl.BlockSpec((1, tq, t), lambda b, i: (b, i, 0)),
        out_shape=jax.ShapeDtypeStruct((bsz, t, t), BF16),
        scratch_shapes=[pltpu.VMEM((nkb, tq, ts), jnp.int32),
                        pltpu.VMEM((tq, 128), jnp.int32)],
        compiler_params=_params("parallel", "arbitrary"),
        name="select_mask",
    )(q_idx, w_idx, k_even, k_odd)


def _t5_bucket_table(n_max):
    n = np.arange(n_max)
    max_exact = REL_BUCKETS // 2
    nf = np.maximum(n, 1).astype(np.float32)
    ratio = np.log(nf / np.float32(max_exact)) / np.float32(math.log(REL_MAX_DIST / max_exact))
    large = max_exact + (ratio * np.float32(REL_BUCKETS - max_exact)).astype(np.int32)
    large = np.minimum(large, REL_BUCKETS - 1)
    return np.where(n < max_exact, n, large).astype(np.int32)


def _bias_kernel(rb_ref, o_ref, *, tq, thresholds):
    i = lax.broadcasted_iota(jnp.int32, (tq, 2 * tq), 0)
    j = lax.broadcasted_iota(jnp.int32, (tq, 2 * tq), 1)
    nh = o_ref.shape[1]
    for v in range(2):
        d = jnp.maximum(v * tq + i - j, 0)
        bucket = jnp.zeros_like(d)
        for th in thresholds:
            bucket = bucket + jnp.where(d >= th, 1, 0)
        for h in range(nh):
            val = jnp.zeros((tq, 2 * tq), F32)
            for b in range(REL_BUCKETS):
                val = jnp.where(bucket == b, rb_ref[b, h] - rb_ref[REL_BUCKETS - 1, h], val)
            o_ref[v, h] = val


def bias_windows(rel_bias, *, tq=QUERY_TILE):
    table = _t5_bucket_table(4 * tq)
    assert np.all(np.diff(table) >= 0) and np.all(np.diff(table) <= 1)
    assert np.all(table[tq:] == REL_BUCKETS - 1), "bias must be constant beyond the near window"
    thresholds = tuple(int(np.argmax(table >= b)) for b in range(1, REL_BUCKETS))
    nh = rel_bias.shape[1]
    return pl.pallas_call(
        functools.partial(_bias_kernel, tq=tq, thresholds=thresholds),
        in_specs=[pl.BlockSpec(memory_space=pltpu.SMEM)],
        out_specs=pl.BlockSpec(memory_space=pltpu.VMEM),
        out_shape=jax.ShapeDtypeStruct((2, nh, tq, 2 * tq), F32),
        compiler_params=pltpu.CompilerParams(vmem_limit_bytes=VMEM_LIMIT),
        name="bias_windows",
    )(rel_bias)


def _attn_kernel(q_ref, madd_ref, ckv_ref, wuk_ref, wuv_ref, btab_ref, o_ref,
                 qlat_ref, m_ref, l_ref, acc_ref, s0_ref, s1_ref, p0_ref, p1_ref, a0_ref, a1_ref,
                 *, tq, nh, dh, fb, scale):
    qi = pl.program_id(1)
    t0 = qi * tq
    for h in range(nh):
        qh = q_ref[0, :, h * dh:(h + 1) * dh]
        qlat_ref[h * tq:(h + 1) * tq, :] = jnp.dot(
            qh, wuk_ref[h], preferred_element_type=F32).astype(BF16)
    m_ref[...] = jnp.full(m_ref.shape, NEG, F32)
    l_ref[...] = jnp.zeros(l_ref.shape, F32)
    acc_ref[...] = jnp.zeros(acc_ref.shape, F32)
    rep = acc_ref.shape[1] // 128
    nw = 2 * tq
    s_ref, p_ref, alpha_ref = (s0_ref, s1_ref), (p0_ref, p1_ref), (a0_ref, a1_ref)

    def qk(kv, slot):
        s_ref[slot][:, :kv.shape[0]] = lax.dot_general(
            qlat_ref[...], kv, _NT, preferred_element_type=F32)

    def softmax(slot, n, ma, bias_of_head):
        for h in range(nh):
            rows = slice(h * tq, (h + 1) * tq)
            lg = s_ref[slot][rows, :n] * scale + ma
            if bias_of_head is not None:
                lg = lg + bias_of_head(h)
            m_old = m_ref[rows, :1]
            m_new = jnp.maximum(m_old, jnp.max(lg, axis=1, keepdims=True))
            alpha = jnp.exp(m_old - m_new)
            p = jnp.exp(lg - m_new)
            l_new = alpha * l_ref[rows, :1] + jnp.sum(p, axis=1, keepdims=True)
            p_ref[slot][rows, :n] = p.astype(BF16)
            alpha_ref[slot][rows, :] = jnp.broadcast_to(alpha, (tq, 128))
            m_ref[rows, :] = jnp.broadcast_to(m_new, (tq, 128))
            l_ref[rows, :] = jnp.broadcast_to(l_new, (tq, 128))

    def pv(kv, slot):
        n = kv.shape[0]
        upd = jnp.dot(p_ref[slot][:, :n], kv, preferred_element_type=F32)
        acc_ref[...] = acc_ref[...] * jnp.concatenate([alpha_ref[slot][...]] * rep, axis=1) + upd

    lim = t0 - tq
    nfar = (jnp.maximum(lim, 0) + fb - 1) // fb
    scol = lax.broadcasted_iota(jnp.int32, (tq, fb), 1)

    def far_kv(kb):
        return ckv_ref[0, pl.ds(pl.multiple_of(kb * fb, fb), fb), :]

    w0 = pl.multiple_of(jnp.maximum(lim, 0), tq)
    variant = jnp.minimum(qi, 1)
    kv_near = ckv_ref[0, pl.ds(w0, nw), :]
    qk(kv_near, 1)
    softmax(1, nw, madd_ref[0, :, pl.ds(w0, nw)].astype(F32), lambda h: btab_ref[variant, h])
    pv(kv_near, 1)
    qk(far_kv(0), 0)
    p_ref[1][...] = jnp.zeros(p_ref[1].shape, BF16)
    alpha_ref[1][...] = jnp.ones(alpha_ref[1].shape, F32)

    def far_stages(kb, slot):
        s0 = pl.multiple_of(kb * fb, fb)
        pv(far_kv(jnp.maximum(kb - 1, 0)), 1 - slot)
        qk(far_kv(jnp.minimum(kb + 1, nfar - 1)), 1 - slot)
        ma = madd_ref[0, :, pl.ds(s0, fb)].astype(F32)
        softmax(slot, fb, jnp.where(s0 + scol < lim, ma, NEG), None)

    def far_body(kb, c):
        for slot in range(2):
            pl.when(kb % 2 == slot)(functools.partial(far_stages, kb, slot))
        return c

    lax.fori_loop(0, nfar, far_body, 0)
    last = jnp.maximum(nfar - 1, 0)
    last_slot = jnp.where(nfar > 0, last % 2, 1)
    for slot in range(2):
        pl.when(last_slot == slot)(functools.partial(pv, far_kv(last), slot))

    for h in range(nh):
        rows = slice(h * tq, (h + 1) * tq)
        o = acc_ref[rows, :] / l_ref[rows, :1]
        y = jnp.dot(o.astype(BF16), wuv_ref[h], preferred_element_type=F32)
        o_ref[0, :, h * dh:(h + 1) * dh] = y.astype(o_ref.dtype)


def masked_latent_attention(q, madd, c_kv, w_uk, w_uv, btab, *, tq=QUERY_TILE, fb=FAR_BLOCK):
    bsz, t, hd = q.shape
    nh, dh, r = w_uk.shape
    fb = _row_tile(t, fb)
    assert t >= fb >= 2 * tq and t % tq == 0
    once = pl.Buffered(1)
    return pl.pallas_call(
        functools.partial(_attn_kernel, tq=tq, nh=nh, dh=dh, fb=fb, scale=float(dh ** -0.5)),
        grid=(bsz, t // tq),
        in_specs=[pl.BlockSpec((1, tq, hd), lambda b, i: (b, i, 0)),
                  pl.BlockSpec((1, tq, t), lambda b, i: (b, i, 0)),
                  pl.BlockSpec((1, t, r), lambda b, i: (b, 0, 0), pipeline_mode=once),
                  pl.BlockSpec((nh, dh, r), lambda b, i: (0, 0, 0), pipeline_mode=once),
                  pl.BlockSpec((nh, r, dh), lambda b, i: (0, 0, 0), pipeline_mode=once),
                  pl.BlockSpec((2, nh, tq, 2 * tq), lambda b, i: (0, 0, 0, 0), pipeline_mode=once)],
        out_specs=pl.BlockSpec((1, tq, hd), lambda b, i: (b, i, 0)),
        out_shape=jax.ShapeDtypeStruct((bsz, t, hd), BF16),
        scratch_shapes=[pltpu.VMEM((nh * tq, r), BF16),
                        pltpu.VMEM((nh * tq, 128), F32),
                        pltpu.VMEM((nh * tq, 128), F32),
                        pltpu.VMEM((nh * tq, r), F32),
                        pltpu.VMEM((nh * tq, fb), F32),
                        pltpu.VMEM((nh * tq, fb), F32),
                        pltpu.VMEM((nh * tq, fb), BF16),
                        pltpu.VMEM((nh * tq, fb), BF16),
                        pltpu.VMEM((nh * tq, 128), F32),
                        pltpu.VMEM((nh * tq, 128), F32)],
        compiler_params=_params("parallel", "arbitrary"),
        name="masked_latent_attention",
    )(q, madd, c_kv, w_uk, w_uv, btab)


def _ffn(x, h, w_in, w_out, lead, next_gain):
    dff = w_out.shape[-2]
    a = dual_matmul(h, WSlab(w_in, lead, cols=dff), WSlab(w_in, lead, col0=dff, cols=dff), BF16,
                    act="swiglu", tm=2048, tn=256, single_buffer_a=True)
    return matmul_residual([(a, WSlab(w_out, lead))], x, 0.5, tm=512, tn=512, next_gain=next_gain)


def _mixer_ab(x, h, bsz, t, i, w_in, ln_a_g, ln_a_b, w_sp, b_sp, conv_w, conv_b, ln_b_g, ln_b_b, w_out,
              next_gain):
    wa = ln_a_g.shape[0]
    wb = ln_b_g.shape[0]
    uv = matmul(h, WSlab(w_in, (i,), cols=2 * wa), F32, act="gelu", tm=1024, tn=512)
    y_a = spatial_gate(uv, ln_a_g, ln_a_b, w_sp, b_sp)
    glu = dual_matmul(h, WSlab(w_in, (i,), col0=2 * wa, cols=wb),
                      WSlab(w_in, (i,), col0=2 * wa + wb, cols=wb), F32, act="glu", tm=1024, tn=256)
    conv = causal_conv(glu.reshape(bsz, t, wb), conv_w, conv_b).reshape(bsz * t, wb)
    y_b = ln_silu(conv, ln_b_g, ln_b_b, BF16)
    return matmul_residual([(y_a, WSlab(w_out, (i,), rows=wa)),
                            (y_b, WSlab(w_out, (i,), row0=wa, rows=wb))], x, 1.0, tm=1024, tn=512,
                           next_gain=next_gain)


def _mixer_cd(x, h, bsz, t, i, w_in, w_pool, pool_scale, g_cq, w_uq, w_qidx, g_ckv, w_uk, w_uv,
              rel_bias, w_out, next_gain):
    n = bsz * t
    wc = pool_scale.shape[0]
    qr = g_cq.shape[0]
    kvr = g_ckv.shape[0]
    n_heads, dh, _ = w_uk.shape
    idx_dim = 64
    idx_heads = w_qidx.shape[-1] // idx_dim
    o3 = wc + qr + kvr
    assert wc % qr == 0 and (wc + qr) % kvr == 0 and idx_heads <= 128 and 2 * idx_dim == 128

    z = matmul(h, WSlab(w_in, (i,), cols=o3), F32, tm=1024, tn=512)
    wk = w_in[i, :, o3:o3 + idx_dim]
    ww = w_in[i, :, o3 + idx_dim:]
    zk = jnp.zeros_like(wk)
    w_small = jnp.concatenate(
        [wk, zk, zk, wk, ww, jnp.zeros((wk.shape[0], 128 - idx_heads), wk.dtype)], axis=1)
    zs = matmul(h, WSlab(w_small), F32, tm=1024, tn=384)
    k_even = zs[:, :128].astype(BF16).reshape(bsz, t, 128)
    k_odd = zs[:, 128:256].astype(BF16).reshape(bsz, t, 128)
    w_idx = zs[:, 256:].reshape(bsz, t, 128)

    y_c = multiscale_pool(z.reshape(bsz, t, o3), w_pool.astype(BF16), pool_scale, width=wc)
    c_q = Act(rms_norm(z, g_cq, BF16, col_block=wc // qr, width=qr))
    c_kv = rms_norm(z, g_ckv, BF16, col_block=(wc + qr) // kvr, width=kvr)
    q = matmul(c_q, WSlab(w_uq, (i,)), BF16, tm=1024, tn=512)
    q_idx = matmul(c_q, WSlab(w_qidx, (i,)), BF16, tm=1024, tn=512)

    topk = min(TOPK_MAX, t // 4)
    madd = select_mask(q_idx.reshape(bsz, t, -1), w_idx, k_even, k_odd, n_heads=idx_heads, topk=topk)
    btab = bias_windows(rel_bias)
    y_d = masked_latent_attention(q.reshape(bsz, t, -1), madd, c_kv.reshape(bsz, t, kvr),
                                  w_uk.astype(BF16), w_uv.astype(BF16), btab)
    return matmul_residual([(y_c.reshape(n, wc), WSlab(w_out, (i,), rows=wc)),
                            (y_d.reshape(n, -1), WSlab(w_out, (i,), row0=wc, rows=n_heads * dh))],
                           x, 1.0, tm=1024, tn=512, next_gain=next_gain)


def kernel(x, g_ff, w_ff_in, w_ff_out, g_mix, w_in_ab, ln_a_g, ln_a_b, w_sp, b_sp, conv_w, conv_b,
           ln_b_g, ln_b_b, w_out_ab, w_in_cd, w_pool, pool_scale, g_cq, w_uq, w_qidx, g_ckv, w_uk,
           w_uv, rel_bias, w_out_cd, g_final):
    bsz, t, d = x.shape
    depth = g_ff.shape[0]
    xs = x.reshape(bsz * t, d)
    w_ff_out = w_ff_out.astype(BF16)
    h = Act(rms_norm(xs, g_ff[0, 0], BF16))
    for l in range(depth):
        xs, h = _ffn(xs, h, w_ff_in, w_ff_out, (l, 0), g_mix[l])
        i = l // 2
        if l % 2 == 0:
            xs, h = _mixer_ab(xs, h, bsz, t, i, w_in_ab, ln_a_g[i], ln_a_b[i], w_sp[i], b_sp[i],
                              conv_w[i], conv_b[i], ln_b_g[i], ln_b_b[i], w_out_ab, g_ff[l, 1])
        else:
            xs, h = _mixer_cd(xs, h, bsz, t, i, w_in_cd, w_pool[i], pool_scale[i], g_cq[i], w_uq,
                              w_qidx, g_ckv[i], w_uk[i], w_uv[i], rel_bias, w_out_cd, g_ff[l, 1])
        xs, h = _ffn(xs, h, w_ff_in, w_ff_out, (l, 1), g_ff[l + 1, 0] if l + 1 < depth else None)
    return rms_norm(xs, g_final, x.dtype).reshape(bsz, t, d)
```

```python
import functools
import math
from typing import NamedTuple, Optional

import numpy as np
import jax
import jax.numpy as jnp
from jax import lax
from jax.experimental import pallas as pl
from jax.experimental.pallas import tpu as pltpu

F32 = jnp.float32
BF16 = jnp.bfloat16

EPS = 1e-6
CHUNK = 128
CONV_WIDTH = 31
POOL_WINDOWS = (2, 4, 8, 16)
TOPK_MAX = 256
REL_BUCKETS = 32
REL_MAX_DIST = 128
QUERY_TILE = 128
FAR_BLOCK = 512
NEG = -1e30
INT_MIN = -2147483648

V7X_VMEM_BYTES = 64 * 1024 * 1024
VMEM_LIMIT = V7X_VMEM_BYTES - 8 * 1024 * 1024

_NT = (((1,), (1,)), ((), ()))


def _params(*sem):
    return pltpu.CompilerParams(dimension_semantics=sem, vmem_limit_bytes=VMEM_LIMIT)


def _row_tile(n, want):
    t = min(n, want)
    assert n % t == 0, (n, t)
    return t


def _rms_kernel(x_ref, g_ref, o_ref):
    x = x_ref[...].astype(F32)
    y = x * lax.rsqrt(jnp.mean(x * x, axis=-1, keepdims=True) + EPS)
    o_ref[...] = (y * g_ref[...]).astype(o_ref.dtype)


def rms_norm(x, g, out_dtype, *, col_block=0, width=None, tm=256):
    n = x.shape[0]
    width = x.shape[1] if width is None else width
    tm = _row_tile(n, tm)
    return pl.pallas_call(
        _rms_kernel,
        grid=(n // tm,),
        in_specs=[pl.BlockSpec((tm, width), lambda i: (i, col_block)),
                  pl.BlockSpec((1, width), lambda i: (0, 0))],
        out_specs=pl.BlockSpec((tm, width), lambda i: (i, 0)),
        out_shape=jax.ShapeDtypeStruct((n, width), out_dtype),
        compiler_params=_params("parallel"),
        name="rms_norm",
    )(x, g.reshape(1, width).astype(F32))


def _ln_silu_kernel(x_ref, g_ref, b_ref, o_ref):
    x = x_ref[...]
    mu = jnp.mean(x, axis=-1, keepdims=True)
    xc = x - mu
    var = jnp.mean(xc * xc, axis=-1, keepdims=True)
    y = xc * lax.rsqrt(var + EPS) * g_ref[...] + b_ref[...]
    o_ref[...] = (y * jax.nn.sigmoid(y)).astype(o_ref.dtype)


def ln_silu(x, g, b, out_dtype, *, tm=256):
    n, d = x.shape
    tm = _row_tile(n, tm)
    return pl.pallas_call(
        _ln_silu_kernel,
        grid=(n // tm,),
        in_specs=[pl.BlockSpec((tm, d), lambda i: (i, 0)),
                  pl.BlockSpec((1, d), lambda i: (0, 0)),
                  pl.BlockSpec((1, d), lambda i: (0, 0))],
        out_specs=pl.BlockSpec((tm, d), lambda i: (i, 0)),
        out_shape=jax.ShapeDtypeStruct((n, d), out_dtype),
        compiler_params=_params("parallel"),
        name="ln_silu",
    )(x, g.reshape(1, d), b.reshape(1, d))


class WSlab(NamedTuple):
    arr: jax.Array
    lead: tuple = ()
    row0: int = 0
    rows: Optional[int] = None
    col0: int = 0
    cols: Optional[int] = None

    def shape(self):
        r = self.arr.shape[-2] if self.rows is None else self.rows
        c = self.arr.shape[-1] if self.cols is None else self.cols
        return r, c

    def spec(self, tn):
        r, _ = self.shape()
        assert self.row0 % r == 0 and self.col0 % tn == 0, (self.row0, r, self.col0, tn)
        lead, rb, cb = self.lead, self.row0 // r, self.col0 // tn
        return pl.BlockSpec((None,) * len(lead) + (r, tn), lambda i, j: lead + (rb, cb + j))


class Act(NamedTuple):
    a: jax.Array
    ssq: Optional[jax.Array] = None

    def specs(self, tm, single_buffer=False):
        k = self.a.shape[1]
        mode = dict(pipeline_mode=pl.Buffered(1)) if single_buffer else {}
        sp = [pl.BlockSpec((tm, k), lambda i, j: (i, 0), **mode)]
        if self.ssq is not None:
            sp.append(pl.BlockSpec((tm, 128), lambda i, j: (i, 0)))
        return sp

    def args(self):
        return [self.a] if self.ssq is None else [self.a, self.ssq]


MXU_ROWS = 256


def _sub_rows(tm):
    return max(min(tm, 2 * MXU_ROWS), tm // 4)


def _row_factor(ssq_ref, rows, k, width):
    rf = lax.rsqrt(ssq_ref[rows, :] * (1.0 / k) + EPS)
    return jnp.concatenate([rf] * (width // 128), axis=1)


def _mm_kernel(*refs, act, normed):
    a_ref, w_ref, o_ref = refs[0], refs[-2], refs[-1]
    w = w_ref[...].astype(BF16)
    tm, k = a_ref.shape
    rb = _sub_rows(tm)
    for r0 in range(0, tm, rb):
        rows = slice(r0, r0 + rb)
        r = jnp.dot(a_ref[rows, :], w, preferred_element_type=F32)
        if normed:
            r = r * _row_factor(refs[1], rows, k, r.shape[1])
        if act == "gelu":
            r = 0.5 * r * (1.0 + lax.erf(r * math.sqrt(0.5)))
        o_ref[rows, :] = r.astype(o_ref.dtype)


def matmul(x, w, out_dtype, *, act=None, tm=1024, tn=512):
    n, k = x.a.shape
    assert w.shape()[0] == k
    nout = w.shape()[1]
    tm = _row_tile(n, tm)
    tn = _row_tile(nout, tn)
    return pl.pallas_call(
        functools.partial(_mm_kernel, act=act, normed=x.ssq is not None),
        grid=(n // tm, nout // tn),
        in_specs=x.specs(tm) + [w.spec(tn)],
        out_specs=pl.BlockSpec((tm, tn), lambda i, j: (i, j)),
        out_shape=jax.ShapeDtypeStruct((n, nout), out_dtype),
        compiler_params=_params("parallel", "arbitrary"),
        name="matmul_" + (act or "plain"),
    )(*x.args(), w.arr)


def _dual_mm_kernel(*refs, act, normed, side_cast):
    if side_cast:
        refs[-1][...] = refs[-3][...].astype(BF16)
        refs = refs[:-3] + (refs[-2],)
    a_ref, w1_ref, w2_ref, o_ref = refs[0], refs[-3], refs[-2], refs[-1]
    w1 = w1_ref[...].astype(BF16)
    w2 = w2_ref[...].astype(BF16)
    tm = a_ref.shape[0]
    rb = _sub_rows(tm)
    for r0 in range(0, tm, rb):
        rows = slice(r0, r0 + rb)
        a = a_ref[rows, :]
        p = jnp.dot(a, w1, preferred_element_type=F32)
        q = jnp.dot(a, w2, preferred_element_type=F32)
        if normed:
            rf = _row_factor(refs[1], rows, a_ref.shape[1], p.shape[1])
            p, q = p * rf, q * rf
        if act == "swiglu":
            r = p * jax.nn.sigmoid(p) * q
        else:
            r = p * jax.nn.sigmoid(q)
        o_ref[rows, :] = r.astype(o_ref.dtype)


def dual_matmul(x, w1, w2, out_dtype, *, act, tm=1024, tn=256, single_buffer_a=False, side_cast=None):
    n, k = x.a.shape
    assert w1.shape() == w2.shape() and w1.shape()[0] == k
    nout = w1.shape()[1]
    tm = _row_tile(n, tm)
    tn = _row_tile(nout, tn)
    ni, nj = n // tm, nout // tn
    in_specs = x.specs(tm, single_buffer_a) + [w1.spec(tn), w2.spec(tn)]
    args = x.args() + [w1.arr, w2.arr]
    out_specs = pl.BlockSpec((tm, tn), lambda i, j: (i, j))
    out_shape = jax.ShapeDtypeStruct((n, nout), out_dtype)
    if side_cast is not None:
        sr, sc = side_cast.shape()
        rp = sr // (ni * nj)
        assert rp * ni * nj == sr and rp % 16 == 0, (sr, ni, nj)
        lead = side_cast.lead
        in_specs.append(pl.BlockSpec((None,) * len(lead) + (rp, sc), lambda i, j: lead + (i * nj + j, 0)))
        args.append(side_cast.arr)
        out_specs = [out_specs, pl.BlockSpec((rp, sc), lambda i, j: (i * nj + j, 0))]
        out_shape = [out_shape, jax.ShapeDtypeStruct((sr, sc), BF16)]
    return pl.pallas_call(
        functools.partial(_dual_mm_kernel, act=act, normed=x.ssq is not None,
                          side_cast=side_cast is not None),
        grid=(ni, nj),
        in_specs=in_specs,
        out_specs=out_specs,
        out_shape=out_shape,
        compiler_params=_params("parallel", "arbitrary"),
        name="dual_matmul_" + act,
    )(*args)


def _mm_res_kernel(*refs, npairs, scale, emit_norm):
    res_ref = refs[2 * npairs]
    if emit_norm:
        g_ref, o_ref, xg_ref, ssq_ref = refs[2 * npairs + 1:]
    else:
        o_ref = refs[2 * npairs + 1]
    tm, tn = o_ref.shape
    ws = [refs[2 * p + 1][...].astype(BF16) for p in range(npairs)]
    rb = _sub_rows(tm)
    parts = []
    for r0 in range(0, tm, rb):
        rows = slice(r0, r0 + rb)
        acc = jnp.dot(refs[0][rows, :], ws[0], preferred_element_type=F32)
        for p in range(1, npairs):
            acc = acc + jnp.dot(refs[2 * p][rows, :], ws[p], preferred_element_type=F32)
        y = res_ref[rows, :] + scale * acc
        o_ref[rows, :] = y
        if emit_norm:
            xg_ref[rows, :] = (y * g_ref[...]).astype(xg_ref.dtype)
            parts.append(jnp.broadcast_to(jnp.sum(y * y, axis=1, keepdims=True), (rb, ssq_ref.shape[1])))
    if not emit_norm:
        return
    part = jnp.concatenate(parts, axis=0)
    j = pl.program_id(1)

    @pl.when(j == 0)
    def _():
        ssq_ref[...] = part

    @pl.when(j > 0)
    def _():
        ssq_ref[...] += part


def matmul_residual(pairs, res, scale, *, tm, tn, next_gain=None):
    n, nout = res.shape
    tm = _row_tile(n, tm)
    tn = _row_tile(nout, tn)
    in_specs, args = [], []
    for a, w in pairs:
        k = a.shape[1]
        assert w.shape() == (k, nout)
        in_specs += [pl.BlockSpec((tm, k), lambda i, j: (i, 0)), w.spec(tn)]
        args += [a, w.arr]
    in_specs.append(pl.BlockSpec((tm, tn), lambda i, j: (i, j)))
    args.append(res)
    out_specs = pl.BlockSpec((tm, tn), lambda i, j: (i, j))
    out_shape = jax.ShapeDtypeStruct((n, nout), F32)
    if next_gain is not None:
        in_specs.append(pl.BlockSpec((1, tn), lambda i, j: (0, j)))
        args.append(next_gain.reshape(1, nout).astype(F32))
        out_specs = [out_specs, pl.BlockSpec((tm, tn), lambda i, j: (i, j)),
                     pl.BlockSpec((tm, 128), lambda i, j: (i, 0))]
        out_shape = [out_shape, jax.ShapeDtypeStruct((n, nout), BF16),
                     jax.ShapeDtypeStruct((n, 128), F32)]
    out = pl.pallas_call(
        functools.partial(_mm_res_kernel, npairs=len(pairs), scale=scale,
                          emit_norm=next_gain is not None),
        grid=(n // tm, nout // tn),
        in_specs=in_specs,
        out_specs=out_specs,
        out_shape=out_shape,
        compiler_params=_params("parallel", "arbitrary"),
        name="matmul_residual",
    )(*args)
    if next_gain is None:
        return out, None
    return out[0], Act(out[1], out[2])


def _spatial_gate_kernel(u_ref, v_ref, g_ref, b_ref, wsp_ref, bspt_ref, o_ref, *, groups, rows):
    v = v_ref[...]
    mu = jnp.mean(v, axis=-1, keepdims=True)
    vc = v - mu
    var = jnp.mean(vc * vc, axis=-1, keepdims=True)
    vn = (vc * lax.rsqrt(var + EPS) * g_ref[...] + b_ref[...]).astype(BF16)
    gw = vn.shape[1] // groups
    ri = lax.broadcasted_iota(jnp.int32, (CHUNK, CHUNK), 0)
    ci = lax.broadcasted_iota(jnp.int32, (CHUNK, CHUNK), 1)
    tril = ci <= ri
    bspt = bspt_ref[...]
    for g in range(groups):
        wm = jnp.where(tril, wsp_ref[g], 0.0).astype(BF16)
        bias = bspt[:, g:g + 1]
        for c in range(rows // CHUNK):
            rs = slice(c * CHUNK, (c + 1) * CHUNK)
            cs = slice(g * gw, (g + 1) * gw)
            s = jnp.dot(wm, vn[rs, cs], preferred_element_type=F32) + bias
            o_ref[rs, cs] = (u_ref[rs, cs] * s).astype(o_ref.dtype)


def spatial_gate(uv, ln_g, ln_b, w_sp, b_sp, *, rows=512):
    n = uv.shape[0]
    w = uv.shape[1] // 2
    groups = w_sp.shape[0]
    rows = _row_tile(n, rows)
    return pl.pallas_call(
        functools.partial(_spatial_gate_kernel, groups=groups, rows=rows),
        grid=(n // rows,),
        in_specs=[pl.BlockSpec((rows, w), lambda i: (i, 0)),
                  pl.BlockSpec((rows, w), lambda i: (i, 1)),
                  pl.BlockSpec((1, w), lambda i: (0, 0)),
                  pl.BlockSpec((1, w), lambda i: (0, 0)),
                  pl.BlockSpec((groups, CHUNK, CHUNK), lambda i: (0, 0, 0)),
                  pl.BlockSpec((CHUNK, groups), lambda i: (0, 0))],
        out_specs=pl.BlockSpec((rows, w), lambda i: (i, 0)),
        out_shape=jax.ShapeDtypeStruct((n, w), BF16),
        compiler_params=_params("parallel"),
        name="spatial_gate",
    )(uv, uv, ln_g.reshape(1, w), ln_b.reshape(1, w), w_sp, b_sp.T)


HALO = 32


SUBLANES = 8


def _conv_kernel(cur_ref, halo_ref, w_ref, b_ref, o_ref, xs_ref, *, tt, rb):
    t = pl.program_id(1)
    halo = halo_ref[0]
    xs_ref[0, 0:HALO, :] = jnp.where(t == 0, jnp.zeros_like(halo), halo)
    xs_ref[0, HALO:HALO + tt, :] = cur_ref[0]
    rows = HALO + tt - SUBLANES
    for rho in range(1, SUBLANES):
        for r0 in range(0, rows, rb):
            nr = min(rb, rows - r0)
            xs_ref[rho, r0:r0 + nr, :] = xs_ref[0, r0 + rho:r0 + rho + nr, :]
    bias = b_ref[...]
    base = HALO - (CONV_WIDTH - 1)
    for r in range(tt // rb):
        acc = jnp.zeros((rb, bias.shape[1]), F32) + bias
        for k in range(CONV_WIDTH):
            rho, m = (base + k) % SUBLANES, (base + k) // SUBLANES
            acc = acc + xs_ref[rho, pl.ds(r * rb + SUBLANES * m, rb), :] * w_ref[pl.ds(k, 1), :]
        o_ref[0, r * rb:(r + 1) * rb, :] = acc


def causal_conv(x, w, b, *, tt=512, cb=256, rb=64):
    bsz, t, c = x.shape
    tt = _row_tile(t, tt)
    return pl.pallas_call(
        functools.partial(_conv_kernel, tt=tt, rb=rb),
        grid=(bsz, t // tt, c // cb),
        in_specs=[pl.BlockSpec((1, tt, cb), lambda bi, ti, ci: (bi, ti, ci)),
                  pl.BlockSpec((1, HALO, cb),
                               lambda bi, ti, ci: (bi, jnp.maximum(ti * (tt // HALO) - 1, 0), ci)),
                  pl.BlockSpec((CONV_WIDTH, cb), lambda bi, ti, ci: (0, ci)),
                  pl.BlockSpec((1, cb), lambda bi, ti, ci: (0, ci))],
        out_specs=pl.BlockSpec((1, tt, cb), lambda bi, ti, ci: (bi, ti, ci)),
        out_shape=jax.ShapeDtypeStruct((bsz, t, c), F32),
        scratch_shapes=[pltpu.VMEM((SUBLANES, HALO + tt, cb), F32)],
        compiler_params=_params("parallel", "parallel", "parallel"),
        name="causal_conv",
    )(x, x, w, b.reshape(1, c))


PHALO = 16


def _pool_kernel(cur_ref, halo_ref, w_ref, sc_ref, o_ref, xpad_ref, *, tt, gw):
    t = pl.program_id(1)
    halo = halo_ref[0]
    xpad_ref[0:PHALO, :] = jnp.where(t == 0, jnp.zeros_like(halo), halo)
    xpad_ref[PHALO:PHALO + tt, :] = cur_ref[0]
    pos = t * tt + lax.broadcasted_iota(jnp.int32, (tt, 1), 0)
    for g, win in enumerate(POOL_WINDOWS):
        cs = slice(g * gw, (g + 1) * gw)
        p = xpad_ref[PHALO:PHALO + tt, cs]
        s = p
        for j in range(1, win):
            s = s + xpad_ref[PHALO - j:PHALO - j + tt, cs]
        cnt = jnp.minimum(pos + 1, win).astype(F32)
        d = (s / cnt - p).astype(BF16)
        y = jnp.dot(d, w_ref[g], preferred_element_type=F32) * sc_ref[:, cs]
        o_ref[0, :, cs] = y.astype(o_ref.dtype)


def multiscale_pool(z, w_pool, scale, *, width, tt=256):
    bsz, t, _ = z.shape
    groups = len(POOL_WINDOWS)
    gw = width // groups
    tt = _row_tile(t, tt)
    return pl.pallas_call(
        functools.partial(_pool_kernel, tt=tt, gw=gw),
        grid=(bsz, t // tt),
        in_specs=[pl.BlockSpec((1, tt, width), lambda bi, ti: (bi, ti, 0)),
                  pl.BlockSpec((1, PHALO, width),
                               lambda bi, ti: (bi, jnp.maximum(ti * (tt // PHALO) - 1, 0), 0)),
                  pl.BlockSpec((groups, gw, gw), lambda bi, ti: (0, 0, 0)),
                  pl.BlockSpec((1, width), lambda bi, ti: (0, 0))],
        out_specs=pl.BlockSpec((1, tt, width), lambda bi, ti: (bi, ti, 0)),
        out_shape=jax.ShapeDtypeStruct((bsz, t, width), BF16),
        scratch_shapes=[pltpu.VMEM((PHALO + tt, width), F32)],
        compiler_params=_params("parallel", "parallel"),
        name="multiscale_pool",
    )(z, z, w_pool, scale.reshape(1, width))


def _select_kernel(q_ref, w_ref, ke_ref, ko_ref, o_ref, key_ref, jmax_ref, *, tq, ts, nkb, npair, topk, wscale):
    qi = pl.program_id(1)
    t0 = qi * tq
    nsb = (t0 + tq + ts - 1) // ts
    w = w_ref[0] * wscale
    trow = t0 + lax.broadcasted_iota(jnp.int32, (tq, ts), 0)
    scol = lax.broadcasted_iota(jnp.int32, (tq, ts), 1)

    def score_body(kb, c):
        s0 = pl.multiple_of(kb * ts, ts)
        ke = ke_ref[0, pl.ds(s0, ts), :]
        ko = ko_ref[0, pl.ds(s0, ts), :]
        acc = jnp.zeros((tq, ts), F32)
        for j in range(npair):
            slab = q_ref[0, :, j * 128:(j + 1) * 128]
            le = lax.dot_general(slab, ke, _NT, preferred_element_type=F32)
            lo = lax.dot_general(slab, ko, _NT, preferred_element_type=F32)
            acc = acc + jnp.maximum(le, 0.0) * w[:, 2 * j:2 * j + 1]
            acc = acc + jnp.maximum(lo, 0.0) * w[:, 2 * j + 1:2 * j + 2]
        acc = jnp.where(s0 + scol <= trow, acc + 0.0, -jnp.inf)
        bits = pltpu.bitcast(acc, jnp.int32)
        key_ref[kb] = jnp.where(bits < 0, bits ^ jnp.int32(0x7FFFFFFF), bits)
        return c

    lax.fori_loop(0, nsb, score_body, 0)

    def count_rows(pred_of_block):
        def count_body(kb, cnt):
            for l in range(ts // 128):
                hit = pred_of_block(key_ref[kb, :, l * 128:(l + 1) * 128], kb * ts + l * 128)
                cnt = cnt + jnp.where(hit, 1.0, 0.0)
            return cnt

        cnt = lax.fori_loop(0, nsb, count_body, jnp.zeros((tq, 128), F32))
        return jnp.broadcast_to(jnp.sum(cnt, axis=1, keepdims=True), (tq, 128))

    def bit_body(it, carry):
        tu, n_ge = carry
        bit = jnp.left_shift(jnp.int32(1), 31 - it)
        cand_u = tu | bit
        cand = cand_u ^ jnp.int32(INT_MIN)
        total = count_rows(lambda k, s0: k >= cand)
        keep = total >= float(topk)
        return jnp.where(keep, cand_u, tu), jnp.where(keep, total, n_ge)

    n_all = jnp.full((tq, 128), 1.0, F32) * (nsb * ts).astype(F32)
    tu, n_ge = lax.fori_loop(0, 32, bit_body, (jnp.zeros((tq, 128), jnp.int32), n_all))
    thr128 = tu ^ jnp.int32(INT_MIN)
    thr = jnp.concatenate([thr128] * (ts // 128), axis=1)

    jmax_ref[...] = jnp.full((tq, 128), nkb * ts, jnp.int32)
    lane = lax.broadcasted_iota(jnp.int32, (tq, 128), 1)

    @pl.when(jnp.max(n_ge) > float(topk))
    def _():
        n_eq = count_rows(lambda k, s0: k == thr128)
        need = float(topk) - (n_ge - n_eq)

        nbits = (nkb * ts - 1).bit_length()

        def idx_body(it, x):
            cand = x | jnp.left_shift(jnp.int32(1), nbits - 1 - it)
            below = count_rows(lambda k, s0: (k == thr128) & (s0 + lane < cand))
            return jnp.where(below < need, cand, x)

        jmax_ref[...] = lax.fori_loop(0, nbits, idx_body, jnp.zeros((tq, 128), jnp.int32))

    jmax = jnp.concatenate([jmax_ref[...]] * (ts // 128), axis=1)

    for kb in range(nkb):
        cols = slice(kb * ts, (kb + 1) * ts)

        @pl.when(kb < nsb)
        def _():
            key = key_ref[kb]
            s = kb * ts + scol
            tie = jnp.where(s <= jmax, 0.0, NEG)
            val = jnp.where(key > thr, 0.0, jnp.where(key == thr, tie, NEG))
            o_ref[0, :, cols] = jnp.where(s <= trow, val, NEG).astype(o_ref.dtype)

        @pl.when(kb >= nsb)
        def _():
            o_ref[0, :, cols] = jnp.full((tq, ts), NEG, o_ref.dtype)


def select_mask(q_idx, w_idx, k_even, k_odd, *, n_heads, topk, tq=QUERY_TILE, ts=FAR_BLOCK):
    bsz, t, hd = q_idx.shape
    ts = _row_tile(t, ts)
    nkb = t // ts
    npair = hd // 128
    wscale = float(n_heads ** -0.5 * (hd // n_heads) ** -0.5)
    return pl.pallas_call(
        functools.partial(_select_kernel, tq=tq, ts=ts, nkb=nkb, npair=npair, topk=topk, wscale=wscale),
        grid=(bsz, t // tq),
        in_specs=[pl.BlockSpec((1, tq, hd), lambda b, i: (b, i, 0)),
                  pl.BlockSpec((1, tq, 128), lambda b, i: (b, i, 0)),
                  pl.BlockSpec((1, t, 128), lambda b, i: (b, 0, 0)),
                  pl.BlockSpec((1, t, 128), lambda b, i: (b, 0, 0))],
        out_specs=pl.BlockSpec((1, tq, t), lambda b, i: (b, i, 0)),
        out_shape=jax.ShapeDtypeStruct((bsz, t, t), BF16),
        scratch_shapes=[pltpu.VMEM((nkb, tq, ts), jnp.int32),
                        pltpu.VMEM((tq, 128), jnp.int32)],
        compiler_params=_params("parallel", "arbitrary"),
        name="select_mask",
    )(q_idx, w_idx, k_even, k_odd)


def _t5_bucket_table(n_max):
    n = np.arange(n_max)
    max_exact = REL_BUCKETS // 2
    nf = np.maximum(n, 1).astype(np.float32)
    ratio = np.log(nf / np.float32(max_exact)) / np.float32(math.log(REL_MAX_DIST / max_exact))
    large = max_exact + (ratio * np.float32(REL_BUCKETS - max_exact)).astype(np.int32)
    large = np.minimum(large, REL_BUCKETS - 1)
    return np.where(n < max_exact, n, large).astype(np.int32)


def _bias_kernel(rb_ref, o_ref, *, tq, thresholds):
    i = lax.broadcasted_iota(jnp.int32, (tq, 2 * tq), 0)
    j = lax.broadcasted_iota(jnp.int32, (tq, 2 * tq), 1)
    nh = o_ref.shape[1]
    for v in range(2):
        d = jnp.maximum(v * tq + i - j, 0)
        bucket = jnp.zeros_like(d)
        for th in thresholds:
            bucket = bucket + jnp.where(d >= th, 1, 0)
        for h in range(nh):
            val = jnp.zeros((tq, 2 * tq), F32)
            for b in range(REL_BUCKETS):
                val = jnp.where(bucket == b, rb_ref[b, h] - rb_ref[REL_BUCKETS - 1, h], val)
            o_ref[v, h] = val


def bias_windows(rel_bias, *, tq=QUERY_TILE):
    table = _t5_bucket_table(4 * tq)
    assert np.all(np.diff(table) >= 0) and np.all(np.diff(table) <= 1)
    assert np.all(table[tq:] == REL_BUCKETS - 1), "bias must be constant beyond the near window"
    thresholds = tuple(int(np.argmax(table >= b)) for b in range(1, REL_BUCKETS))
    nh = rel_bias.shape[1]
    return pl.pallas_call(
        functools.partial(_bias_kernel, tq=tq, thresholds=thresholds),
        in_specs=[pl.BlockSpec(memory_space=pltpu.SMEM)],
        out_specs=pl.BlockSpec(memory_space=pltpu.VMEM),
        out_shape=jax.ShapeDtypeStruct((2, nh, tq, 2 * tq), F32),
        compiler_params=pltpu.CompilerParams(vmem_limit_bytes=VMEM_LIMIT),
        name="bias_windows",
    )(rel_bias)


def _attn_kernel(q_ref, madd_ref, ckv_ref, wuk_ref, wuv_ref, btab_ref, o_ref,
                 qlat_ref, m_ref, l_ref, acc_ref, s0_ref, s1_ref, p0_ref, p1_ref, a0_ref, a1_ref,
                 *, tq, nh, dh, fb, scale):
    qi = pl.program_id(1)
    t0 = qi * tq
    for h in range(nh):
        qh = q_ref[0, :, h * dh:(h + 1) * dh]
        qlat_ref[h * tq:(h + 1) * tq, :] = jnp.dot(
            qh, wuk_ref[h], preferred_element_type=F32).astype(BF16)
    m_ref[...] = jnp.full(m_ref.shape, NEG, F32)
    l_ref[...] = jnp.zeros(l_ref.shape, F32)
    acc_ref[...] = jnp.zeros(acc_ref.shape, F32)
    rep = acc_ref.shape[1] // 128
    nw = 2 * tq
    s_ref, p_ref, alpha_ref = (s0_ref, s1_ref), (p0_ref, p1_ref), (a0_ref, a1_ref)

    def qk(kv, slot):
        s_ref[slot][:, :kv.shape[0]] = lax.dot_general(
            qlat_ref[...], kv, _NT, preferred_element_type=F32)

    def softmax(slot, n, ma, bias_of_head):
        for h in range(nh):
            rows = slice(h * tq, (h + 1) * tq)
            lg = s_ref[slot][rows, :n] * scale + ma
            if bias_of_head is not None:
                lg = lg + bias_of_head(h)
            m_old = m_ref[rows, :1]
            m_new = jnp.maximum(m_old, jnp.max(lg, axis=1, keepdims=True))
            alpha = jnp.exp(m_old - m_new)
            p = jnp.exp(lg - m_new)
            l_new = alpha * l_ref[rows, :1] + jnp.sum(p, axis=1, keepdims=True)
            p_ref[slot][rows, :n] = p.astype(BF16)
            alpha_ref[slot][rows, :] = jnp.broadcast_to(alpha, (tq, 128))
            m_ref[rows, :] = jnp.broadcast_to(m_new, (tq, 128))
            l_ref[rows, :] = jnp.broadcast_to(l_new, (tq, 128))

    def pv(kv, slot):
        n = kv.shape[0]
        upd = jnp.dot(p_ref[slot][:, :n], kv, preferred_element_type=F32)
        acc_ref[...] = acc_ref[...] * jnp.concatenate([alpha_ref[slot][...]] * rep, axis=1) + upd

    lim = t0 - tq
    nfar = (jnp.maximum(lim, 0) + fb - 1) // fb
    scol = lax.broadcasted_iota(jnp.int32, (tq, fb), 1)

    def far_kv(kb):
        return ckv_ref[0, pl.ds(pl.multiple_of(kb * fb, fb), fb), :]

    w0 = pl.multiple_of(jnp.maximum(lim, 0), tq)
    variant = jnp.minimum(qi, 1)
    kv_near = ckv_ref[0, pl.ds(w0, nw), :]
    qk(kv_near, 1)
    softmax(1, nw, madd_ref[0, :, pl.ds(w0, nw)].astype(F32), lambda h: btab_ref[variant, h])
    pv(kv_near, 1)
    qk(far_kv(0), 0)
    p_ref[1][...] = jnp.zeros(p_ref[1].shape, BF16)
    alpha_ref[1][...] = jnp.ones(alpha_ref[1].shape, F32)

    def far_stages(kb, slot):
        s0 = pl.multiple_of(kb * fb, fb)
        pv(far_kv(jnp.maximum(kb - 1, 0)), 1 - slot)
        qk(far_kv(jnp.minimum(kb + 1, nfar - 1)), 1 - slot)
        ma = madd_ref[0, :, pl.ds(s0, fb)].astype(F32)
        softmax(slot, fb, jnp.where(s0 + scol < lim, ma, NEG), None)

    def far_body(kb, c):
        for slot in range(2):
            pl.when(kb % 2 == slot)(functools.partial(far_stages, kb, slot))
        return c

    lax.fori_loop(0, nfar, far_body, 0)
    last = jnp.maximum(nfar - 1, 0)
    last_slot = jnp.where(nfar > 0, last % 2, 1)
    for slot in range(2):
        pl.when(last_slot == slot)(functools.partial(pv, far_kv(last), slot))

    for h in range(nh):
        rows = slice(h * tq, (h + 1) * tq)
        o = acc_ref[rows, :] / l_ref[rows, :1]
        y = jnp.dot(o.astype(BF16), wuv_ref[h], preferred_element_type=F32)
        o_ref[0, :, h * dh:(h + 1) * dh] = y.astype(o_ref.dtype)


def masked_latent_attention(q, madd, c_kv, w_uk, w_uv, btab, *, tq=QUERY_TILE, fb=FAR_BLOCK):
    bsz, t, hd = q.shape
    nh, dh, r = w_uk.shape
    fb = _row_tile(t, fb)
    assert t >= fb >= 2 * tq and t % tq == 0
    once = pl.Buffered(1)
    return pl.pallas_call(
        functools.partial(_attn_kernel, tq=tq, nh=nh, dh=dh, fb=fb, scale=float(dh ** -0.5)),
        grid=(bsz, t // tq),
        in_specs=[pl.BlockSpec((1, tq, hd), lambda b, i: (b, i, 0)),
                  pl.BlockSpec((1, tq, t), lambda b, i: (b, i, 0)),
                  pl.BlockSpec((1, t, r), lambda b, i: (b, 0, 0), pipeline_mode=once),
                  pl.BlockSpec((nh, dh, r), lambda b, i: (0, 0, 0), pipeline_mode=once),
                  pl.BlockSpec((nh, r, dh), lambda b, i: (0, 0, 0), pipeline_mode=once),
                  pl.BlockSpec((2, nh, tq, 2 * tq), lambda b, i: (0, 0, 0, 0), pipeline_mode=once)],
        out_specs=pl.BlockSpec((1, tq, hd), lambda b, i: (b, i, 0)),
        out_shape=jax.ShapeDtypeStruct((bsz, t, hd), BF16),
        scratch_shapes=[pltpu.VMEM((nh * tq, r), BF16),
                        pltpu.VMEM((nh * tq, 128), F32),
                        pltpu.VMEM((nh * tq, 128), F32),
                        pltpu.VMEM((nh * tq, r), F32),
                        pltpu.VMEM((nh * tq, fb), F32),
                        pltpu.VMEM((nh * tq, fb), F32),
                        pltpu.VMEM((nh * tq, fb), BF16),
                        pltpu.VMEM((nh * tq, fb), BF16),
                        pltpu.VMEM((nh * tq, 128), F32),
                        pltpu.VMEM((nh * tq, 128), F32)],
        compiler_params=_params("parallel", "arbitrary"),
        name="masked_latent_attention",
    )(q, madd, c_kv, w_uk, w_uv, btab)


def _ffn(x, h, w_in, w_out, lead, next_gain):
    dff = w_out.shape[-2]
    a, w_out_bf16 = dual_matmul(h, WSlab(w_in, lead, cols=dff), WSlab(w_in, lead, col0=dff, cols=dff), BF16,
                                act="swiglu", tm=2048, tn=256, single_buffer_a=True,
                                side_cast=WSlab(w_out, lead))
    return matmul_residual([(a, WSlab(w_out_bf16))], x, 0.5, tm=512, tn=512, next_gain=next_gain)


def _mixer_ab(x, h, bsz, t, i, w_in, ln_a_g, ln_a_b, w_sp, b_sp, conv_w, conv_b, ln_b_g, ln_b_b, w_out,
              next_gain):
    wa = ln_a_g.shape[0]
    wb = ln_b_g.shape[0]
    uv = matmul(h, WSlab(w_in, (i,), cols=2 * wa), F32, act="gelu", tm=1024, tn=512)
    y_a = spatial_gate(uv, ln_a_g, ln_a_b, w_sp, b_sp)
    glu = dual_matmul(h, WSlab(w_in, (i,), col0=2 * wa, cols=wb),
                      WSlab(w_in, (i,), col0=2 * wa + wb, cols=wb), F32, act="glu", tm=1024, tn=256)
    conv = causal_conv(glu.reshape(bsz, t, wb), conv_w, conv_b).reshape(bsz * t, wb)
    y_b = ln_silu(conv, ln_b_g, ln_b_b, BF16)
    return matmul_residual([(y_a, WSlab(w_out, (i,), rows=wa)),
                            (y_b, WSlab(w_out, (i,), row0=wa, rows=wb))], x, 1.0, tm=1024, tn=512,
                           next_gain=next_gain)


def _mixer_cd(x, h, bsz, t, i, w_in, w_pool, pool_scale, g_cq, w_uq, w_qidx, g_ckv, w_uk, w_uv,
              rel_bias, w_out, next_gain):
    n = bsz * t
    wc = pool_scale.shape[0]
    qr = g_cq.shape[0]
    kvr = g_ckv.shape[0]
    n_heads, dh, _ = w_uk.shape
    idx_dim = 64
    idx_heads = w_qidx.shape[-1] // idx_dim
    o3 = wc + qr + kvr
    assert wc % qr == 0 and (wc + qr) % kvr == 0 and idx_heads <= 128 and 2 * idx_dim == 128

    z = matmul(h, WSlab(w_in, (i,), cols=o3), F32, tm=1024, tn=512)
    wk = w_in[i, :, o3:o3 + idx_dim]
    ww = w_in[i, :, o3 + idx_dim:]
    zk = jnp.zeros_like(wk)
    w_small = jnp.concatenate(
        [wk, zk, zk, wk, ww, jnp.zeros((wk.shape[0], 128 - idx_heads), wk.dtype)], axis=1)
    zs = matmul(h, WSlab(w_small), F32, tm=1024, tn=384)
    k_even = zs[:, :128].astype(BF16).reshape(bsz, t, 128)
    k_odd = zs[:, 128:256].astype(BF16).reshape(bsz, t, 128)
    w_idx = zs[:, 256:].reshape(bsz, t, 128)

    y_c = multiscale_pool(z.reshape(bsz, t, o3), w_pool.astype(BF16), pool_scale, width=wc)
    c_q = Act(rms_norm(z, g_cq, BF16, col_block=wc // qr, width=qr))
    c_kv = rms_norm(z, g_ckv, BF16, col_block=(wc + qr) // kvr, width=kvr)
    q = matmul(c_q, WSlab(w_uq, (i,)), BF16, tm=1024, tn=512)
    q_idx = matmul(c_q, WSlab(w_qidx, (i,)), BF16, tm=1024, tn=512)

    topk = min(TOPK_MAX, t // 4)
    madd = select_mask(q_idx.reshape(bsz, t, -1), w_idx, k_even, k_odd, n_heads=idx_heads, topk=topk)
    btab = bias_windows(rel_bias)
    y_d = masked_latent_attention(q.reshape(bsz, t, -1), madd, c_kv.reshape(bsz, t, kvr),
                                  w_uk.astype(BF16), w_uv.astype(BF16), btab)
    return matmul_residual([(y_c.reshape(n, wc), WSlab(w_out, (i,), rows=wc)),
                            (y_d.reshape(n, -1), WSlab(w_out, (i,), row0=wc, rows=n_heads * dh))],
                           x, 1.0, tm=1024, tn=512, next_gain=next_gain)


def kernel(x, g_ff, w_ff_in, w_ff_out, g_mix, w_in_ab, ln_a_g, ln_a_b, w_sp, b_sp, conv_w, conv_b,
           ln_b_g, ln_b_b, w_out_ab, w_in_cd, w_pool, pool_scale, g_cq, w_uq, w_qidx, g_ckv, w_uk,
           w_uv, rel_bias, w_out_cd, g_final):
    bsz, t, d = x.shape
    depth = g_ff.shape[0]
    xs = x.reshape(bsz * t, d)
    h = Act(rms_norm(xs, g_ff[0, 0], BF16))
    for l in range(depth):
        xs, h = _ffn(xs, h, w_ff_in, w_ff_out, (l, 0), g_mix[l])
        i = l // 2
        if l % 2 == 0:
            xs, h = _mixer_ab(xs, h, bsz, t, i, w_in_ab, ln_a_g[i], ln_a_b[i], w_sp[i], b_sp[i],
                              conv_w[i], conv_b[i], ln_b_g[i], ln_b_b[i], w_out_ab, g_ff[l, 1])
        else:
            xs, h = _mixer_cd(xs, h, bsz, t, i, w_in_cd, w_pool[i], pool_scale[i], g_cq[i], w_uq,
                              w_qidx, g_ckv[i], w_uk[i], w_uv[i], rel_bias, w_out_cd, g_ff[l, 1])
        xs, h = _ffn(xs, h, w_ff_in, w_ff_out, (l, 1), g_ff[l + 1, 0] if l + 1 < depth else None)
    return rms_norm(xs, g_final, x.dtype).reshape(bsz, t, d)
```

```python
import functools
import math
from typing import NamedTuple, Optional

import numpy as np
import jax
import jax.numpy as jnp
from jax import lax
from jax.experimental import pallas as pl
from jax.experimental.pallas import tpu as pltpu

F32 = jnp.float32
BF16 = jnp.bfloat16

EPS = 1e-6
CHUNK = 128
CONV_WIDTH = 31
POOL_WINDOWS = (2, 4, 8, 16)
TOPK_MAX = 256
REL_BUCKETS = 32
REL_MAX_DIST = 128
QUERY_TILE = 128
FAR_BLOCK = 512
NEG = -1e30
INT_MIN = -2147483648

V7X_VMEM_BYTES = 64 * 1024 * 1024
VMEM_LIMIT = V7X_VMEM_BYTES - 8 * 1024 * 1024

_NT = (((1,), (1,)), ((), ()))


def _params(*sem):
    return pltpu.CompilerParams(dimension_semantics=sem, vmem_limit_bytes=VMEM_LIMIT)


def _row_tile(n, want):
    t = min(n, want)
    assert n % t == 0, (n, t)
    return t


def _rms_kernel(x_ref, g_ref, o_ref):
    x = x_ref[...].astype(F32)
    y = x * lax.rsqrt(jnp.mean(x * x, axis=-1, keepdims=True) + EPS)
    o_ref[...] = (y * g_ref[...]).astype(o_ref.dtype)


def rms_norm(x, g, out_dtype, *, col_block=0, width=None, tm=256):
    n = x.shape[0]
    width = x.shape[1] if width is None else width
    tm = _row_tile(n, tm)
    return pl.pallas_call(
        _rms_kernel,
        grid=(n // tm,),
        in_specs=[pl.BlockSpec((tm, width), lambda i: (i, col_block)),
                  pl.BlockSpec((1, width), lambda i: (0, 0))],
        out_specs=pl.BlockSpec((tm, width), lambda i: (i, 0)),
        out_shape=jax.ShapeDtypeStruct((n, width), out_dtype),
        compiler_params=_params("parallel"),
        name="rms_norm",
    )(x, g.reshape(1, width).astype(F32))


def _ln_silu_kernel(x_ref, g_ref, b_ref, o_ref):
    x = x_ref[...]
    mu = jnp.mean(x, axis=-1, keepdims=True)
    xc = x - mu
    var = jnp.mean(xc * xc, axis=-1, keepdims=True)
    y = xc * lax.rsqrt(var + EPS) * g_ref[...] + b_ref[...]
    o_ref[...] = (y * jax.nn.sigmoid(y)).astype(o_ref.dtype)


def ln_silu(x, g, b, out_dtype, *, tm=256):
    n, d = x.shape
    tm = _row_tile(n, tm)
    return pl.pallas_call(
        _ln_silu_kernel,
        grid=(n // tm,),
        in_specs=[pl.BlockSpec((tm, d), lambda i: (i, 0)),
                  pl.BlockSpec((1, d), lambda i: (0, 0)),
                  pl.BlockSpec((1, d), lambda i: (0, 0))],
        out_specs=pl.BlockSpec((tm, d), lambda i: (i, 0)),
        out_shape=jax.ShapeDtypeStruct((n, d), out_dtype),
        compiler_params=_params("parallel"),
        name="ln_silu",
    )(x, g.reshape(1, d), b.reshape(1, d))


class WSlab(NamedTuple):
    arr: jax.Array
    lead: tuple = ()
    row0: int = 0
    rows: Optional[int] = None
    col0: int = 0
    cols: Optional[int] = None

    def shape(self):
        r = self.arr.shape[-2] if self.rows is None else self.rows
        c = self.arr.shape[-1] if self.cols is None else self.cols
        return r, c

    def spec(self, tn):
        r, _ = self.shape()
        assert self.row0 % r == 0 and self.col0 % tn == 0, (self.row0, r, self.col0, tn)
        lead, rb, cb = self.lead, self.row0 // r, self.col0 // tn
        return pl.BlockSpec((None,) * len(lead) + (r, tn), lambda i, j: lead + (rb, cb + j))


class Act(NamedTuple):
    a: jax.Array
    ssq: Optional[jax.Array] = None

    def specs(self, tm, single_buffer=False):
        k = self.a.shape[1]
        mode = dict(pipeline_mode=pl.Buffered(1)) if single_buffer else {}
        sp = [pl.BlockSpec((tm, k), lambda i, j: (i, 0), **mode)]
        if self.ssq is not None:
            sp.append(pl.BlockSpec((tm, 128), lambda i, j: (i, 0)))
        return sp

    def args(self):
        return [self.a] if self.ssq is None else [self.a, self.ssq]


MXU_ROWS = 256


def _sub_rows(tm):
    return max(min(tm, 2 * MXU_ROWS), tm // 4)


def _row_factor(ssq_ref, rows, k, width):
    rf = lax.rsqrt(ssq_ref[rows, :] * (1.0 / k) + EPS)
    return jnp.concatenate([rf] * (width // 128), axis=1)


def _mm_kernel(*refs, act, normed):
    a_ref, w_ref, o_ref = refs[0], refs[-2], refs[-1]
    w = w_ref[...].astype(BF16)
    tm, k = a_ref.shape
    rb = _sub_rows(tm)
    for r0 in range(0, tm, rb):
        rows = slice(r0, r0 + rb)
        r = jnp.dot(a_ref[rows, :], w, preferred_element_type=F32)
        if normed:
            r = r * _row_factor(refs[1], rows, k, r.shape[1])
        if act == "gelu":
            r = 0.5 * r * (1.0 + lax.erf(r * math.sqrt(0.5)))
        o_ref[rows, :] = r.astype(o_ref.dtype)


def matmul(x, w, out_dtype, *, act=None, tm=1024, tn=512):
    n, k = x.a.shape
    assert w.shape()[0] == k
    nout = w.shape()[1]
    tm = _row_tile(n, tm)
    tn = _row_tile(nout, tn)
    return pl.pallas_call(
        functools.partial(_mm_kernel, act=act, normed=x.ssq is not None),
        grid=(n // tm, nout // tn),
        in_specs=x.specs(tm) + [w.spec(tn)],
        out_specs=pl.BlockSpec((tm, tn), lambda i, j: (i, j)),
        out_shape=jax.ShapeDtypeStruct((n, nout), out_dtype),
        compiler_params=_params("parallel", "arbitrary"),
        name="matmul_" + (act or "plain"),
    )(*x.args(), w.arr)


def _dual_mm_kernel(*refs, act, normed, n_side):
    if n_side:
        for t in range(n_side):
            refs[-n_side + t][...] = refs[-(2 * n_side + 1) + t][...].astype(BF16)
        refs = refs[:-(2 * n_side + 1)] + (refs[-(n_side + 1)],)
    a_ref, w1_ref, w2_ref, o_ref = refs[0], refs[-3], refs[-2], refs[-1]
    w1 = w1_ref[...].astype(BF16)
    w2 = w2_ref[...].astype(BF16)
    tm = a_ref.shape[0]
    rb = _sub_rows(tm)
    for r0 in range(0, tm, rb):
        rows = slice(r0, r0 + rb)
        a = a_ref[rows, :]
        p = jnp.dot(a, w1, preferred_element_type=F32)
        q = jnp.dot(a, w2, preferred_element_type=F32)
        if normed:
            rf = _row_factor(refs[1], rows, a_ref.shape[1], p.shape[1])
            p, q = p * rf, q * rf
        if act == "swiglu":
            r = p * jax.nn.sigmoid(p) * q
        else:
            r = p * jax.nn.sigmoid(q)
        o_ref[rows, :] = r.astype(o_ref.dtype)


BF16_ROWS = 16


def dual_matmul(x, w1, w2, out_dtype, *, act, tm=1024, tn=256, single_buffer_a=False, side_casts=()):
    n, k = x.a.shape
    assert w1.shape() == w2.shape() and w1.shape()[0] == k
    nout = w1.shape()[1]
    tm = _row_tile(n, tm)
    tn = _row_tile(nout, tn)
    ni, nj = n // tm, nout // tn
    in_specs = x.specs(tm, single_buffer_a) + [w1.spec(tn), w2.spec(tn)]
    args = x.args() + [w1.arr, w2.arr]
    out_specs = pl.BlockSpec((tm, tn), lambda i, j: (i, j))
    out_shape = jax.ShapeDtypeStruct((n, nout), out_dtype)
    side_casts = list(side_casts)
    if side_casts:
        out_specs, out_shape = [out_specs], [out_shape]
    for sc in side_casts:
        assert sc.rows is None and sc.cols is None
        sr, scols = sc.shape()
        rp = next(r for r in range(BF16_ROWS, sr + 1, BF16_ROWS)
                  if sr % r == 0 and r * ni * nj >= sr)
        last = sr // rp - 1
        lead = sc.lead
        in_specs.append(pl.BlockSpec((None,) * len(lead) + (rp, scols),
                                     lambda i, j, lead=lead, last=last: lead + (jnp.minimum(i * nj + j, last), 0)))
        args.append(sc.arr)
        out_specs.append(pl.BlockSpec((rp, scols), lambda i, j, last=last: (jnp.minimum(i * nj + j, last), 0)))
        out_shape.append(jax.ShapeDtypeStruct((sr, scols), BF16))
    return pl.pallas_call(
        functools.partial(_dual_mm_kernel, act=act, normed=x.ssq is not None,
                          n_side=len(side_casts)),
        grid=(ni, nj),
        in_specs=in_specs,
        out_specs=out_specs,
        out_shape=out_shape,
        compiler_params=_params("parallel", "arbitrary"),
        name="dual_matmul_" + act,
    )(*args)


def _mm_res_kernel(*refs, npairs, scale, emit_norm):
    res_ref = refs[2 * npairs]
    if emit_norm:
        g_ref, o_ref, xg_ref, ssq_ref = refs[2 * npairs + 1:]
    else:
        o_ref = refs[2 * npairs + 1]
    tm, tn = o_ref.shape
    ws = [refs[2 * p + 1][...].astype(BF16) for p in range(npairs)]
    rb = _sub_rows(tm)
    parts = []
    for r0 in range(0, tm, rb):
        rows = slice(r0, r0 + rb)
        acc = jnp.dot(refs[0][rows, :], ws[0], preferred_element_type=F32)
        for p in range(1, npairs):
            acc = acc + jnp.dot(refs[2 * p][rows, :], ws[p], preferred_element_type=F32)
        y = res_ref[rows, :] + scale * acc
        o_ref[rows, :] = y
        if emit_norm:
            xg_ref[rows, :] = (y * g_ref[...]).astype(xg_ref.dtype)
            parts.append(jnp.broadcast_to(jnp.sum(y * y, axis=1, keepdims=True), (rb, ssq_ref.shape[1])))
    if not emit_norm:
        return
    part = jnp.concatenate(parts, axis=0)
    j = pl.program_id(1)

    @pl.when(j == 0)
    def _():
        ssq_ref[...] = part

    @pl.when(j > 0)
    def _():
        ssq_ref[...] += part


def matmul_residual(pairs, res, scale, *, tm, tn, next_gain=None):
    n, nout = res.shape
    tm = _row_tile(n, tm)
    tn = _row_tile(nout, tn)
    in_specs, args = [], []
    for a, w in pairs:
        k = a.shape[1]
        assert w.shape() == (k, nout)
        in_specs += [pl.BlockSpec((tm, k), lambda i, j: (i, 0)), w.spec(tn)]
        args += [a, w.arr]
    in_specs.append(pl.BlockSpec((tm, tn), lambda i, j: (i, j)))
    args.append(res)
    out_specs = pl.BlockSpec((tm, tn), lambda i, j: (i, j))
    out_shape = jax.ShapeDtypeStruct((n, nout), F32)
    if next_gain is not None:
        in_specs.append(pl.BlockSpec((1, tn), lambda i, j: (0, j)))
        args.append(next_gain.reshape(1, nout).astype(F32))
        out_specs = [out_specs, pl.BlockSpec((tm, tn), lambda i, j: (i, j)),
                     pl.BlockSpec((tm, 128), lambda i, j: (i, 0))]
        out_shape = [out_shape, jax.ShapeDtypeStruct((n, nout), BF16),
                     jax.ShapeDtypeStruct((n, 128), F32)]
    out = pl.pallas_call(
        functools.partial(_mm_res_kernel, npairs=len(pairs), scale=scale,
                          emit_norm=next_gain is not None),
        grid=(n // tm, nout // tn),
        in_specs=in_specs,
        out_specs=out_specs,
        out_shape=out_shape,
        compiler_params=_params("parallel", "arbitrary"),
        name="matmul_residual",
    )(*args)
    if next_gain is None:
        return out, None
    return out[0], Act(out[1], out[2])


def _spatial_gate_kernel(u_ref, v_ref, g_ref, b_ref, wsp_ref, bspt_ref, o_ref, *, groups, rows):
    v = v_ref[...]
    mu = jnp.mean(v, axis=-1, keepdims=True)
    vc = v - mu
    var = jnp.mean(vc * vc, axis=-1, keepdims=True)
    vn = (vc * lax.rsqrt(var + EPS) * g_ref[...] + b_ref[...]).astype(BF16)
    gw = vn.shape[1] // groups
    ri = lax.broadcasted_iota(jnp.int32, (CHUNK, CHUNK), 0)
    ci = lax.broadcasted_iota(jnp.int32, (CHUNK, CHUNK), 1)
    tril = ci <= ri
    bspt = bspt_ref[...]
    for g in range(groups):
        wm = jnp.where(tril, wsp_ref[g], 0.0).astype(BF16)
        bias = bspt[:, g:g + 1]
        for c in range(rows // CHUNK):
            rs = slice(c * CHUNK, (c + 1) * CHUNK)
            cs = slice(g * gw, (g + 1) * gw)
            s = jnp.dot(wm, vn[rs, cs], preferred_element_type=F32) + bias
            o_ref[rs, cs] = (u_ref[rs, cs] * s).astype(o_ref.dtype)


def spatial_gate(uv, ln_g, ln_b, w_sp, b_sp, *, rows=512):
    n = uv.shape[0]
    w = uv.shape[1] // 2
    groups = w_sp.shape[0]
    rows = _row_tile(n, rows)
    return pl.pallas_call(
        functools.partial(_spatial_gate_kernel, groups=groups, rows=rows),
        grid=(n // rows,),
        in_specs=[pl.BlockSpec((rows, w), lambda i: (i, 0)),
                  pl.BlockSpec((rows, w), lambda i: (i, 1)),
                  pl.BlockSpec((1, w), lambda i: (0, 0)),
                  pl.BlockSpec((1, w), lambda i: (0, 0)),
                  pl.BlockSpec((groups, CHUNK, CHUNK), lambda i: (0, 0, 0)),
                  pl.BlockSpec((CHUNK, groups), lambda i: (0, 0))],
        out_specs=pl.BlockSpec((rows, w), lambda i: (i, 0)),
        out_shape=jax.ShapeDtypeStruct((n, w), BF16),
        compiler_params=_params("parallel"),
        name="spatial_gate",
    )(uv, uv, ln_g.reshape(1, w), ln_b.reshape(1, w), w_sp, b_sp.T)


HALO = 32


SUBLANES = 8


def _conv_kernel(cur_ref, halo_ref, w_ref, b_ref, o_ref, xs_ref, *, tt, rb):
    t = pl.program_id(1)
    halo = halo_ref[0]
    xs_ref[0, 0:HALO, :] = jnp.where(t == 0, jnp.zeros_like(halo), halo)
    xs_ref[0, HALO:HALO + tt, :] = cur_ref[0]
    rows = HALO + tt - SUBLANES
    for rho in range(1, SUBLANES):
        for r0 in range(0, rows, rb):
            nr = min(rb, rows - r0)
            xs_ref[rho, r0:r0 + nr, :] = xs_ref[0, r0 + rho:r0 + rho + nr, :]
    bias = b_ref[...]
    base = HALO - (CONV_WIDTH - 1)
    for r in range(tt // rb):
        acc = jnp.zeros((rb, bias.shape[1]), F32) + bias
        for k in range(CONV_WIDTH):
            rho, m = (base + k) % SUBLANES, (base + k) // SUBLANES
            acc = acc + xs_ref[rho, pl.ds(r * rb + SUBLANES * m, rb), :] * w_ref[pl.ds(k, 1), :]
        o_ref[0, r * rb:(r + 1) * rb, :] = acc


def causal_conv(x, w, b, *, tt=512, cb=256, rb=64):
    bsz, t, c = x.shape
    tt = _row_tile(t, tt)
    return pl.pallas_call(
        functools.partial(_conv_kernel, tt=tt, rb=rb),
        grid=(bsz, t // tt, c // cb),
        in_specs=[pl.BlockSpec((1, tt, cb), lambda bi, ti, ci: (bi, ti, ci)),
                  pl.BlockSpec((1, HALO, cb),
                               lambda bi, ti, ci: (bi, jnp.maximum(ti * (tt // HALO) - 1, 0), ci)),
                  pl.BlockSpec((CONV_WIDTH, cb), lambda bi, ti, ci: (0, ci)),
                  pl.BlockSpec((1, cb), lambda bi, ti, ci: (0, ci))],
        out_specs=pl.BlockSpec((1, tt, cb), lambda bi, ti, ci: (bi, ti, ci)),
        out_shape=jax.ShapeDtypeStruct((bsz, t, c), F32),
        scratch_shapes=[pltpu.VMEM((SUBLANES, HALO + tt, cb), F32)],
        compiler_params=_params("parallel", "parallel", "parallel"),
        name="causal_conv",
    )(x, x, w, b.reshape(1, c))


PHALO = 16


def _pool_kernel(cur_ref, halo_ref, w_ref, sc_ref, o_ref, xpad_ref, *, tt, gw):
    t = pl.program_id(1)
    halo = halo_ref[0]
    xpad_ref[0:PHALO, :] = jnp.where(t == 0, jnp.zeros_like(halo), halo)
    xpad_ref[PHALO:PHALO + tt, :] = cur_ref[0]
    pos = t * tt + lax.broadcasted_iota(jnp.int32, (tt, 1), 0)
    for g, win in enumerate(POOL_WINDOWS):
        cs = slice(g * gw, (g + 1) * gw)
        p = xpad_ref[PHALO:PHALO + tt, cs]
        s = p
        for j in range(1, win):
            s = s + xpad_ref[PHALO - j:PHALO - j + tt, cs]
        cnt = jnp.minimum(pos + 1, win).astype(F32)
        d = (s / cnt - p).astype(BF16)
        y = jnp.dot(d, w_ref[g], preferred_element_type=F32) * sc_ref[:, cs]
        o_ref[0, :, cs] = y.astype(o_ref.dtype)


def multiscale_pool(z, w_pool, scale, *, width, tt=256):
    bsz, t, _ = z.shape
    groups = len(POOL_WINDOWS)
    gw = width // groups
    tt = _row_tile(t, tt)
    return pl.pallas_call(
        functools.partial(_pool_kernel, tt=tt, gw=gw),
        grid=(bsz, t // tt),
        in_specs=[pl.BlockSpec((1, tt, width), lambda bi, ti: (bi, ti, 0)),
                  pl.BlockSpec((1, PHALO, width),
                               lambda bi, ti: (bi, jnp.maximum(ti * (tt // PHALO) - 1, 0), 0)),
                  pl.BlockSpec((groups, gw, gw), lambda bi, ti: (0, 0, 0)),
                  pl.BlockSpec((1, width), lambda bi, ti: (0, 0))],
        out_specs=pl.BlockSpec((1, tt, width), lambda bi, ti: (bi, ti, 0)),
        out_shape=jax.ShapeDtypeStruct((bsz, t, width), BF16),
        scratch_shapes=[pltpu.VMEM((PHALO + tt, width), F32)],
        compiler_params=_params("parallel", "parallel"),
        name="multiscale_pool",
    )(z, z, w_pool, scale.reshape(1, width))


def _select_kernel(q_ref, w_ref, ke_ref, ko_ref, o_ref, key_ref, jmax_ref, *, tq, ts, nkb, npair, topk, wscale):
    qi = pl.program_id(1)
    t0 = qi * tq
    nsb = (t0 + tq + ts - 1) // ts
    w = w_ref[0] * wscale
    trow = t0 + lax.broadcasted_iota(jnp.int32, (tq, ts), 0)
    scol = lax.broadcasted_iota(jnp.int32, (tq, ts), 1)

    def score_body(kb, c):
        s0 = pl.multiple_of(kb * ts, ts)
        ke = ke_ref[0, pl.ds(s0, ts), :]
        ko = ko_ref[0, pl.ds(s0, ts), :]
        acc = jnp.zeros((tq, ts), F32)
        for j in range(npair):
            slab = q_ref[0, :, j * 128:(j + 1) * 128]
            le = lax.dot_general(slab, ke, _NT, preferred_element_type=F32)
            lo = lax.dot_general(slab, ko, _NT, preferred_element_type=F32)
            acc = acc + jnp.maximum(le, 0.0) * w[:, 2 * j:2 * j + 1]
            acc = acc + jnp.maximum(lo, 0.0) * w[:, 2 * j + 1:2 * j + 2]
        acc = jnp.where(s0 + scol <= trow, acc + 0.0, -jnp.inf)
        bits = pltpu.bitcast(acc, jnp.int32)
        key_ref[kb] = jnp.where(bits < 0, bits ^ jnp.int32(0x7FFFFFFF), bits)
        return c

    lax.fori_loop(0, nsb, score_body, 0)

    def count_rows(pred_of_block):
        def count_body(kb, cnt):
            for l in range(ts // 128):
                hit = pred_of_block(key_ref[kb, :, l * 128:(l + 1) * 128], kb * ts + l * 128)
                cnt = cnt + jnp.where(hit, 1.0, 0.0)
            return cnt

        cnt = lax.fori_loop(0, nsb, count_body, jnp.zeros((tq, 128), F32))
        return jnp.broadcast_to(jnp.sum(cnt, axis=1, keepdims=True), (tq, 128))

    def bit_body(it, carry):
        tu, n_ge = carry
        bit = jnp.left_shift(jnp.int32(1), 31 - it)
        cand_u = tu | bit
        cand = cand_u ^ jnp.int32(INT_MIN)
        total = count_rows(lambda k, s0: k >= cand)
        keep = total >= float(topk)
        return jnp.where(keep, cand_u, tu), jnp.where(keep, total, n_ge)

    n_all = jnp.full((tq, 128), 1.0, F32) * (nsb * ts).astype(F32)
    tu, n_ge = lax.fori_loop(0, 32, bit_body, (jnp.zeros((tq, 128), jnp.int32), n_all))
    thr128 = tu ^ jnp.int32(INT_MIN)
    thr = jnp.concatenate([thr128] * (ts // 128), axis=1)

    jmax_ref[...] = jnp.full((tq, 128), nkb * ts, jnp.int32)
    lane = lax.broadcasted_iota(jnp.int32, (tq, 128), 1)

    @pl.when(jnp.max(n_ge) > float(topk))
    def _():
        n_eq = count_rows(lambda k, s0: k == thr128)
        need = float(topk) - (n_ge - n_eq)

        nbits = (nkb * ts - 1).bit_length()

        def idx_body(it, x):
            cand = x | jnp.left_shift(jnp.int32(1), nbits - 1 - it)
            below = count_rows(lambda k, s0: (k == thr128) & (s0 + lane < cand))
            return jnp.where(below < need, cand, x)

        jmax_ref[...] = lax.fori_loop(0, nbits, idx_body, jnp.zeros((tq, 128), jnp.int32))

    jmax = jnp.concatenate([jmax_ref[...]] * (ts // 128), axis=1)

    for kb in range(nkb):
        cols = slice(kb * ts, (kb + 1) * ts)

        @pl.when(kb < nsb)
        def _():
            key = key_ref[kb]
            s = kb * ts + scol
            tie = jnp.where(s <= jmax, 0.0, NEG)
            val = jnp.where(key > thr, 0.0, jnp.where(key == thr, tie, NEG))
            o_ref[0, :, cols] = jnp.where(s <= trow, val, NEG).astype(o_ref.dtype)

        @pl.when(kb >= nsb)
        def _():
            o_ref[0, :, cols] = jnp.full((tq, ts), NEG, o_ref.dtype)


def select_mask(q_idx, w_idx, k_even, k_odd, *, n_heads, topk, tq=QUERY_TILE, ts=FAR_BLOCK):
    bsz, t, hd = q_idx.shape
    ts = _row_tile(t, ts)
    nkb = t // ts
    npair = hd // 128
    wscale = float(n_heads ** -0.5 * (hd // n_heads) ** -0.5)
    return pl.pallas_call(
        functools.partial(_select_kernel, tq=tq, ts=ts, nkb=nkb, npair=npair, topk=topk, wscale=wscale),
        grid=(bsz, t // tq),
        in_specs=[pl.BlockSpec((1, tq, hd), lambda b, i: (b, i, 0)),
                  pl.BlockSpec((1, tq, 128), lambda b, i: (b, i, 0)),
                  pl.BlockSpec((1, t, 128), lambda b, i: (b, 0, 0)),
                  pl.BlockSpec((1, t, 128), lambda b, i: (b, 0, 0))],
        out_specs=pl.BlockSpec((1, tq, t), lambda b, i: (b, i, 0)),
        out_shape=jax.ShapeDtypeStruct((bsz, t, t), BF16),
        scratch_shapes=[pltpu.VMEM((nkb, tq, ts), jnp.int32),
                        pltpu.VMEM((tq, 128), jnp.int32)],
        compiler_params=_params("parallel", "arbitrary"),
        name="select_mask",
    )(q_idx, w_idx, k_even, k_odd)


def _t5_bucket_table(n_max):
    n = np.arange(n_max)
    max_exact = REL_BUCKETS // 2
    nf = np.maximum(n, 1).astype(np.float32)
    ratio = np.log(nf / np.float32(max_exact)) / np.float32(math.log(REL_MAX_DIST / max_exact))
    large = max_exact + (ratio * np.float32(REL_BUCKETS - max_exact)).astype(np.int32)
    large = np.minimum(large, REL_BUCKETS - 1)
    return np.where(n < max_exact, n, large).astype(np.int32)


def _bias_kernel(rb_ref, o_ref, *, tq, thresholds):
    i = lax.broadcasted_iota(jnp.int32, (tq, 2 * tq), 0)
    j = lax.broadcasted_iota(jnp.int32, (tq, 2 * tq), 1)
    nh = o_ref.shape[1]
    for v in range(2):
        d = jnp.maximum(v * tq + i - j, 0)
        bucket = jnp.zeros_like(d)
        for th in thresholds:
            bucket = bucket + jnp.where(d >= th, 1, 0)
        for h in range(nh):
            val = jnp.zeros((tq, 2 * tq), F32)
            for b in range(REL_BUCKETS):
                val = jnp.where(bucket == b, rb_ref[b, h] - rb_ref[REL_BUCKETS - 1, h], val)
            o_ref[v, h] = val


def bias_windows(rel_bias, *, tq=QUERY_TILE):
    table = _t5_bucket_table(4 * tq)
    assert np.all(np.diff(table) >= 0) and np.all(np.diff(table) <= 1)
    assert np.all(table[tq:] == REL_BUCKETS - 1), "bias must be constant beyond the near window"
    thresholds = tuple(int(np.argmax(table >= b)) for b in range(1, REL_BUCKETS))
    nh = rel_bias.shape[1]
    return pl.pallas_call(
        functools.partial(_bias_kernel, tq=tq, thresholds=thresholds),
        in_specs=[pl.BlockSpec(memory_space=pltpu.SMEM)],
        out_specs=pl.BlockSpec(memory_space=pltpu.VMEM),
        out_shape=jax.ShapeDtypeStruct((2, nh, tq, 2 * tq), F32),
        compiler_params=pltpu.CompilerParams(vmem_limit_bytes=VMEM_LIMIT),
        name="bias_windows",
    )(rel_bias)


def _attn_kernel(q_ref, madd_ref, ckv_ref, wuk_ref, wuv_ref, btab_ref, o_ref,
                 qlat_ref, m_ref, l_ref, acc_ref, s0_ref, s1_ref, p0_ref, p1_ref, a0_ref, a1_ref,
                 *, tq, nh, dh, fb, scale):
    qi = pl.program_id(1)
    t0 = qi * tq
    for h in range(nh):
        qh = q_ref[0, :, h * dh:(h + 1) * dh]
        qlat_ref[h * tq:(h + 1) * tq, :] = jnp.dot(
            qh, wuk_ref[h], preferred_element_type=F32).astype(BF16)
    m_ref[...] = jnp.full(m_ref.shape, NEG, F32)
    l_ref[...] = jnp.zeros(l_ref.shape, F32)
    acc_ref[...] = jnp.zeros(acc_ref.shape, F32)
    rep = acc_ref.shape[1] // 128
    nw = 2 * tq
    s_ref, p_ref, alpha_ref = (s0_ref, s1_ref), (p0_ref, p1_ref), (a0_ref, a1_ref)

    def qk(kv, slot):
        s_ref[slot][:, :kv.shape[0]] = lax.dot_general(
            qlat_ref[...], kv, _NT, preferred_element_type=F32)

    def softmax(slot, n, ma, bias_of_head):
        for h in range(nh):
            rows = slice(h * tq, (h + 1) * tq)
            lg = s_ref[slot][rows, :n] * scale + ma
            if bias_of_head is not None:
                lg = lg + bias_of_head(h)
            m_old = m_ref[rows, :1]
            m_new = jnp.maximum(m_old, jnp.max(lg, axis=1, keepdims=True))
            alpha = jnp.exp(m_old - m_new)
            p = jnp.exp(lg - m_new)
            l_new = alpha * l_ref[rows, :1] + jnp.sum(p, axis=1, keepdims=True)
            p_ref[slot][rows, :n] = p.astype(BF16)
            alpha_ref[slot][rows, :] = jnp.broadcast_to(alpha, (tq, 128))
            m_ref[rows, :] = jnp.broadcast_to(m_new, (tq, 128))
            l_ref[rows, :] = jnp.broadcast_to(l_new, (tq, 128))

    def pv(kv, slot):
        n = kv.shape[0]
        upd = jnp.dot(p_ref[slot][:, :n], kv, preferred_element_type=F32)
        acc_ref[...] = acc_ref[...] * jnp.concatenate([alpha_ref[slot][...]] * rep, axis=1) + upd

    lim = t0 - tq
    nfar = (jnp.maximum(lim, 0) + fb - 1) // fb
    scol = lax.broadcasted_iota(jnp.int32, (tq, fb), 1)

    def far_kv(kb):
        return ckv_ref[0, pl.ds(pl.multiple_of(kb * fb, fb), fb), :]

    w0 = pl.multiple_of(jnp.maximum(lim, 0), tq)
    variant = jnp.minimum(qi, 1)
    kv_near = ckv_ref[0, pl.ds(w0, nw), :]
    qk(kv_near, 1)
    softmax(1, nw, madd_ref[0, :, pl.ds(w0, nw)].astype(F32), lambda h: btab_ref[variant, h])
    pv(kv_near, 1)
    qk(far_kv(0), 0)
    p_ref[1][...] = jnp.zeros(p_ref[1].shape, BF16)
    alpha_ref[1][...] = jnp.ones(alpha_ref[1].shape, F32)

    def far_stages(kb, slot):
        s0 = pl.multiple_of(kb * fb, fb)
        pv(far_kv(jnp.maximum(kb - 1, 0)), 1 - slot)
        qk(far_kv(jnp.minimum(kb + 1, nfar - 1)), 1 - slot)
        ma = madd_ref[0, :, pl.ds(s0, fb)].astype(F32)
        softmax(slot, fb, jnp.where(s0 + scol < lim, ma, NEG), None)

    def far_body(kb, c):
        for slot in range(2):
            pl.when(kb % 2 == slot)(functools.partial(far_stages, kb, slot))
        return c

    lax.fori_loop(0, nfar, far_body, 0)
    last = jnp.maximum(nfar - 1, 0)
    last_slot = jnp.where(nfar > 0, last % 2, 1)
    for slot in range(2):
        pl.when(last_slot == slot)(functools.partial(pv, far_kv(last), slot))

    for h in range(nh):
        rows = slice(h * tq, (h + 1) * tq)
        o = acc_ref[rows, :] / l_ref[rows, :1]
        y = jnp.dot(o.astype(BF16), wuv_ref[h], preferred_element_type=F32)
        o_ref[0, :, h * dh:(h + 1) * dh] = y.astype(o_ref.dtype)


def masked_latent_attention(q, madd, c_kv, w_uk, w_uv, btab, *, tq=QUERY_TILE, fb=FAR_BLOCK):
    bsz, t, hd = q.shape
    nh, dh, r = w_uk.shape
    fb = _row_tile(t, fb)
    assert t >= fb >= 2 * tq and t % tq == 0
    once = pl.Buffered(1)
    return pl.pallas_call(
        functools.partial(_attn_kernel, tq=tq, nh=nh, dh=dh, fb=fb, scale=float(dh ** -0.5)),
        grid=(bsz, t // tq),
        in_specs=[pl.BlockSpec((1, tq, hd), lambda b, i: (b, i, 0)),
                  pl.BlockSpec((1, tq, t), lambda b, i: (b, i, 0)),
                  pl.BlockSpec((1, t, r), lambda b, i: (b, 0, 0), pipeline_mode=once),
                  pl.BlockSpec((nh, dh, r), lambda b, i: (0, 0, 0), pipeline_mode=once),
                  pl.BlockSpec((nh, r, dh), lambda b, i: (0, 0, 0), pipeline_mode=once),
                  pl.BlockSpec((2, nh, tq, 2 * tq), lambda b, i: (0, 0, 0, 0), pipeline_mode=once)],
        out_specs=pl.BlockSpec((1, tq, hd), lambda b, i: (b, i, 0)),
        out_shape=jax.ShapeDtypeStruct((bsz, t, hd), BF16),
        scratch_shapes=[pltpu.VMEM((nh * tq, r), BF16),
                        pltpu.VMEM((nh * tq, 128), F32),
                        pltpu.VMEM((nh * tq, 128), F32),
                        pltpu.VMEM((nh * tq, r), F32),
                        pltpu.VMEM((nh * tq, fb), F32),
                        pltpu.VMEM((nh * tq, fb), F32),
                        pltpu.VMEM((nh * tq, fb), BF16),
                        pltpu.VMEM((nh * tq, fb), BF16),
                        pltpu.VMEM((nh * tq, 128), F32),
                        pltpu.VMEM((nh * tq, 128), F32)],
        compiler_params=_params("parallel", "arbitrary"),
        name="masked_latent_attention",
    )(q, madd, c_kv, w_uk, w_uv, btab)


def _ffn(x, h, w_in, w_out, lead, next_gain, also_cast=()):
    dff = w_out.shape[-2]
    a, w_out_bf16, *copies = dual_matmul(
        h, WSlab(w_in, lead, cols=dff), WSlab(w_in, lead, col0=dff, cols=dff), BF16,
        act="swiglu", tm=2048, tn=256, single_buffer_a=True,
        side_casts=[WSlab(w_out, lead)] + list(also_cast))
    y, h_next = matmul_residual([(a, WSlab(w_out_bf16))], x, 0.5, tm=512, tn=512, next_gain=next_gain)
    return y, h_next, copies


def _slab(stacked, i, copy, **window):
    return WSlab(copy, (), **window) if copy is not None else WSlab(stacked, (i,), **window)


def _mixer_ab(x, h, bsz, t, i, w_in, ln_a_g, ln_a_b, w_sp, b_sp, conv_w, conv_b, ln_b_g, ln_b_b, w_out,
              next_gain, w_in_c=None, w_out_c=None):
    wa = ln_a_g.shape[0]
    wb = ln_b_g.shape[0]
    uv = matmul(h, _slab(w_in, i, w_in_c, cols=2 * wa), F32, act="gelu", tm=1024, tn=512)
    y_a = spatial_gate(uv, ln_a_g, ln_a_b, w_sp, b_sp)
    glu = dual_matmul(h, _slab(w_in, i, w_in_c, col0=2 * wa, cols=wb),
                      _slab(w_in, i, w_in_c, col0=2 * wa + wb, cols=wb), F32, act="glu", tm=1024, tn=256)
    conv = causal_conv(glu.reshape(bsz, t, wb), conv_w, conv_b).reshape(bsz * t, wb)
    y_b = ln_silu(conv, ln_b_g, ln_b_b, BF16)
    return matmul_residual([(y_a, _slab(w_out, i, w_out_c, rows=wa)),
                            (y_b, _slab(w_out, i, w_out_c, row0=wa, rows=wb))], x, 1.0, tm=1024, tn=512,
                           next_gain=next_gain)


def _mixer_cd(x, h, bsz, t, i, w_in, w_pool, pool_scale, g_cq, w_uq, w_qidx, g_ckv, w_uk, w_uv,
              rel_bias, w_out, next_gain, w_in_c=None, w_out_c=None):
    n = bsz * t
    wc = pool_scale.shape[0]
    qr = g_cq.shape[0]
    kvr = g_ckv.shape[0]
    n_heads, dh, _ = w_uk.shape
    idx_dim = 64
    idx_heads = w_qidx.shape[-1] // idx_dim
    o3 = wc + qr + kvr
    assert wc % qr == 0 and (wc + qr) % kvr == 0 and idx_heads <= 128 and 2 * idx_dim == 128

    z = matmul(h, _slab(w_in, i, w_in_c, cols=o3), F32, tm=1024, tn=512)
    wk = w_in[i, :, o3:o3 + idx_dim]
    ww = w_in[i, :, o3 + idx_dim:]
    zk = jnp.zeros_like(wk)
    w_small = jnp.concatenate(
        [wk, zk, zk, wk, ww, jnp.zeros((wk.shape[0], 128 - idx_heads), wk.dtype)], axis=1)
    zs = matmul(h, WSlab(w_small), F32, tm=1024, tn=384)
    k_even = zs[:, :128].astype(BF16).reshape(bsz, t, 128)
    k_odd = zs[:, 128:256].astype(BF16).reshape(bsz, t, 128)
    w_idx = zs[:, 256:].reshape(bsz, t, 128)

    y_c = multiscale_pool(z.reshape(bsz, t, o3), w_pool.astype(BF16), pool_scale, width=wc)
    c_q = Act(rms_norm(z, g_cq, BF16, col_block=wc // qr, width=qr))
    c_kv = rms_norm(z, g_ckv, BF16, col_block=(wc + qr) // kvr, width=kvr)
    q = matmul(c_q, WSlab(w_uq, (i,)), BF16, tm=1024, tn=512)
    q_idx = matmul(c_q, WSlab(w_qidx, (i,)), BF16, tm=1024, tn=512)

    topk = min(TOPK_MAX, t // 4)
    madd = select_mask(q_idx.reshape(bsz, t, -1), w_idx, k_even, k_odd, n_heads=idx_heads, topk=topk)
    btab = bias_windows(rel_bias)
    y_d = masked_latent_attention(q.reshape(bsz, t, -1), madd, c_kv.reshape(bsz, t, kvr),
                                  w_uk.astype(BF16), w_uv.astype(BF16), btab)
    return matmul_residual([(y_c.reshape(n, wc), _slab(w_out, i, w_out_c, rows=wc)),
                            (y_d.reshape(n, -1), _slab(w_out, i, w_out_c, row0=wc, rows=n_heads * dh))],
                           x, 1.0, tm=1024, tn=512, next_gain=next_gain)


def kernel(x, g_ff, w_ff_in, w_ff_out, g_mix, w_in_ab, ln_a_g, ln_a_b, w_sp, b_sp, conv_w, conv_b,
           ln_b_g, ln_b_b, w_out_ab, w_in_cd, w_pool, pool_scale, g_cq, w_uq, w_qidx, g_ckv, w_uk,
           w_uv, rel_bias, w_out_cd, g_final):
    bsz, t, d = x.shape
    depth = g_ff.shape[0]
    xs = x.reshape(bsz * t, d)
    h = Act(rms_norm(xs, g_ff[0, 0], BF16))
    for l in range(depth):
        i = l // 2
        w_in_mix, w_out_mix = (w_in_ab, w_out_ab) if l % 2 == 0 else (w_in_cd, w_out_cd)
        xs, h, (w_in_c, w_out_c) = _ffn(xs, h, w_ff_in, w_ff_out, (l, 0), g_mix[l],
                                        also_cast=[WSlab(w_in_mix, (i,)), WSlab(w_out_mix, (i,))])
        if l % 2 == 0:
            xs, h = _mixer_ab(xs, h, bsz, t, i, w_in_ab, ln_a_g[i], ln_a_b[i], w_sp[i], b_sp[i],
                              conv_w[i], conv_b[i], ln_b_g[i], ln_b_b[i], w_out_ab, g_ff[l, 1],
                              w_in_c, w_out_c)
        else:
            xs, h = _mixer_cd(xs, h, bsz, t, i, w_in_cd, w_pool[i], pool_scale[i], g_cq[i], w_uq,
                              w_qidx, g_ckv[i], w_uk[i], w_uv[i], rel_bias, w_out_cd, g_ff[l, 1],
                              w_in_c, w_out_c)
        xs, h, _ = _ffn(xs, h, w_ff_in, w_ff_out, (l, 1), g_ff[l + 1, 0] if l + 1 < depth else None)
    return rms_norm(xs, g_final, x.dtype).reshape(bsz, t, d)
```
